```python
import jax
import jax.numpy as jnp
from jax import lax
import numpy as np

D_MODEL = 1024
BATCH = 2
SEQ = 8192
DEPTH = 4
DEC_BATCH = 128
DEC_SEQ = 1
PAST_LEN = 8192
PAGE_SIZE = 128

N_MIXERS = 3
N_A = (DEPTH + N_MIXERS - 1) // N_MIXERS
N_B = (DEPTH + N_MIXERS - 2) // N_MIXERS
N_C = (DEPTH + N_MIXERS - 3) // N_MIXERS

A_HEADS = 4
A_DK = D_MODEL // 8
A_DV = D_MODEL // 4
MLSTM_CHUNK = 64
F_BIAS_LO = 3.0
F_BIAS_HI = 6.0
M_INIT = -1e30

B_HEADS = 16
B_NOPE = 64
B_ROPE = 32
B_VDIM = 64
B_Q_LORA = 768
B_KV_LORA = 256
ROPE_THETA = 10000.0
MLA_SCALE = (B_NOPE + B_ROPE) ** -0.5

C_HEADS = 16
C_HEAD_DIM = D_MODEL // C_HEADS
C_SCALE = C_HEAD_DIM ** -0.5

D_FF = ((8 * D_MODEL + 3 * 256 - 1) // (3 * 256)) * 256
QBLOCK = 128
EPS = 1e-6

kernel_name = 'hybrid_mlstm_mla_stickbreak_decode_step'


def rms_norm(x, g):
    xf = x.astype(jnp.float32)
    y = xf * lax.rsqrt(jnp.mean(xf * xf, axis=-1, keepdims=True) + EPS)
    return (y * g.astype(jnp.float32)).astype(x.dtype)


def swiglu(h, w_in, w_out):
    gu = h @ w_in
    return (jax.nn.silu(gu[..., :D_FF]) * gu[..., D_FF:]) @ w_out


def rope(x, pos):
    half = x.shape[-1] // 2
    inv_freq = ROPE_THETA ** (-jnp.arange(half, dtype=jnp.float32) / half)
    ang = pos.astype(jnp.float32)[:, None] * inv_freq[None, :]
    shape = (1, pos.shape[0]) + (1,) * (x.ndim - 3) + (half,)
    cos = jnp.cos(ang).reshape(shape)
    sin = jnp.sin(ang).reshape(shape)
    x1, x2 = x[..., :half], x[..., half:]
    return jnp.concatenate([x1 * cos - x2 * sin, x2 * cos + x1 * sin], axis=-1).astype(x.dtype)


def gather_pages(pool, layer, page_table):
    g = pool[layer, page_table]
    return g.reshape((g.shape[0], g.shape[1] * g.shape[2]) + g.shape[3:])


def mlstm_chunkwise(q, k, v, log_i, log_f, c0, n0, m0, chunk):
    bsz, seq, heads, _ = q.shape
    n_chunks = seq // chunk

    def to_chunks(x):
        x = x.astype(jnp.float32).reshape((bsz, n_chunks, chunk) + x.shape[2:])
        return jnp.moveaxis(x, (1, 3), (0, 2))

    causal = jnp.tril(jnp.ones((chunk, chunk), dtype=bool))

    def step(carry, inp):
        c_mem, n_mem, m_prev = carry
        qc, kc, vc, li, lf = inp
        b = jnp.cumsum(lf, axis=-1)
        log_d = jnp.where(causal, b[..., :, None] - b[..., None, :] + li[..., None, :], -jnp.inf)
        log_inter = b + m_prev[..., None]
        m_t = jnp.maximum(log_inter, jnp.max(log_d, axis=-1))
        s = jnp.einsum('bhtd,bhsd->bhts', qc, kc) * jnp.exp(log_d - m_t[..., None])
        w_inter = jnp.exp(log_inter - m_t)
        num = w_inter[..., None] * jnp.einsum('bhtd,bhde->bhte', qc, c_mem) + jnp.einsum('bhts,bhse->bhte', s, vc)
        den = w_inter * jnp.einsum('bhtd,bhd->bht', qc, n_mem) + jnp.sum(s, axis=-1)
        h = num / jnp.maximum(jnp.abs(den), jnp.exp(-m_t))[..., None]
        m_new = m_t[..., -1]
        w_row = jnp.exp(b[..., -1:] - b + li - m_new[..., None])
        decay = jnp.exp(b[..., -1] + m_prev - m_new)
        c_new = decay[..., None, None] * c_mem + jnp.einsum('bhsd,bhse->bhde', kc * w_row[..., None], vc)
        n_new = decay[..., None] * n_mem + jnp.einsum('bhs,bhsd->bhd', w_row, kc)
        return (c_new, n_new, m_new), h

    init = (c0.astype(jnp.float32), n0.astype(jnp.float32), m0.astype(jnp.float32))
    (c_fin, n_fin, m_fin), hs = lax.scan(
        step, init, (to_chunks(q), to_chunks(k), to_chunks(v), to_chunks(log_i), to_chunks(log_f)))
    hs = jnp.moveaxis(hs, (0, 2), (1, 3)).reshape(bsz, seq, heads, v.shape[-1])
    return hs, c_fin, n_fin, m_fin


def mlstm_layer(h, c0, n0, m0, w_in, b_gate, g_head, w_out, chunk):
    bsz, seq, _ = h.shape
    hk, hv = A_HEADS * A_DK, A_HEADS * A_DV
    p = h @ w_in
    q = p[..., :hk].reshape(bsz, seq, A_HEADS, A_DK)
    k = p[..., hk:2 * hk].reshape(bsz, seq, A_HEADS, A_DK) * (A_DK ** -0.5)
    v = p[..., 2 * hk:2 * hk + hv].reshape(bsz, seq, A_HEADS, A_DV)
    o = p[..., 2 * hk + hv:2 * hk + 2 * hv]
    gates = (p[..., 2 * hk + 2 * hv:] + b_gate).astype(jnp.float32)
    log_i = gates[..., :A_HEADS]
    log_f = jax.nn.log_sigmoid(gates[..., A_HEADS:])
    hh, c_fin, n_fin, m_fin = mlstm_chunkwise(q, k, v, log_i, log_f, c0, n0, m0, chunk)
    hn = rms_norm(hh, g_head.reshape(A_HEADS, A_DV)).reshape(bsz, seq, hv)
    y = jax.nn.sigmoid(o.astype(jnp.float32)) * hn
    return (y @ w_out).astype(h.dtype), c_fin, n_fin, m_fin


def mla_project(h, pos, w_down, g_q, g_kv, w_uq):
    bsz, seq, _ = h.shape
    d = h @ w_down
    c_q = rms_norm(d[..., :B_Q_LORA], g_q)
    c_kv = rms_norm(d[..., B_Q_LORA:B_Q_LORA + B_KV_LORA], g_kv)
    k_rope = rope(d[..., B_Q_LORA + B_KV_LORA:], pos)
    q = (c_q @ w_uq).reshape(bsz, seq, B_HEADS, B_NOPE + B_ROPE)
    return q[..., :B_NOPE], rope(q[..., B_NOPE:], pos), c_kv, k_rope


def mla_layer_prompt(h, pos, w_down, g_q, g_kv, w_uq, w_uk, w_uv, w_o):
    bsz, seq, _ = h.shape
    q_nope, q_rope, c_kv, k_rope = mla_project(h, pos, w_down, g_q, g_kv, w_uq)
    k_nope = jnp.einsum('bsc,chd->bshd', c_kv, w_uk)
    v = jnp.einsum('bsc,chd->bshd', c_kv, w_uv)
    n_blocks = seq // QBLOCK
    qn = q_nope.reshape(bsz, n_blocks, QBLOCK, B_HEADS, B_NOPE).swapaxes(0, 1)
    qr = q_rope.reshape(bsz, n_blocks, QBLOCK, B_HEADS, B_ROPE).swapaxes(0, 1)
    kpos = jnp.arange(seq)

    def block(args):
        qn_b, qr_b, start = args
        s = (jnp.einsum('bqhd,bkhd->bhqk', qn_b, k_nope)
             + jnp.einsum('bqhr,bkr->bhqk', qr_b, k_rope)).astype(jnp.float32) * MLA_SCALE
        qpos = start + jnp.arange(QBLOCK)
        p = jax.nn.softmax(jnp.where(kpos[None, :] <= qpos[:, None], s, -jnp.inf), axis=-1)
        return jnp.einsum('bhqk,bkhd->bqhd', p, v)

    o = lax.map(block, (qn, qr, jnp.arange(n_blocks) * QBLOCK))
    o = o.swapaxes(0, 1).reshape(bsz, seq, B_HEADS * B_VDIM)
    return (o @ w_o).astype(h.dtype), c_kv, k_rope


def mla_layer_sample(h, pos, pool_latent, pool_krope, layer, page_table,
                     w_down, g_q, g_kv, w_uq, w_uk, w_uv, w_o):
    bsz, t_new, _ = h.shape
    q_nope, q_rope, c_kv, k_rope = mla_project(h, pos, w_down, g_q, g_kv, w_uq)
    q_lat = jnp.einsum('bthd,chd->bthc', q_nope, w_uk)
    lat_past = gather_pages(pool_latent, layer, page_table)
    kr_past = gather_pages(pool_krope, layer, page_table)
    n_past = lat_past.shape[1]
    s_past = jnp.einsum('bthc,bsc->bhts', q_lat, lat_past) + jnp.einsum('bthr,bsr->bhts', q_rope, kr_past)
    s_new = jnp.einsum('bthc,bsc->bhts', q_lat, c_kv) + jnp.einsum('bthr,bsr->bhts', q_rope, k_rope)
    s = jnp.concatenate([s_past, s_new], axis=-1).astype(jnp.float32) * MLA_SCALE
    qpos = n_past + jnp.arange(t_new)
    kpos = jnp.arange(n_past + t_new)
    p = jax.nn.softmax(jnp.where(kpos[None, :] <= qpos[:, None], s, -jnp.inf), axis=-1)
    o_lat = (jnp.einsum('bhts,bsc->bthc', p[..., :n_past], lat_past)
             + jnp.einsum('bhts,bsc->bthc', p[..., n_past:], c_kv))
    o = jnp.einsum('bthc,chd->bthd', o_lat, w_uv).reshape(bsz, t_new, B_HEADS * B_VDIM)
    return (o @ w_o).astype(h.dtype), c_kv, k_rope


def stick_breaking_weights(z, valid):
    log_beta = jax.nn.log_sigmoid(z)
    log_rest = jnp.where(valid, jax.nn.log_sigmoid(-z), 0.0)
    tail = lax.cumsum(log_rest, axis=z.ndim - 1, reverse=True) - log_rest
    return jnp.where(valid, jnp.exp(log_beta + tail), 0.0)


def sb_project(h, w_qkv):
    bsz, seq, _ = h.shape
    qkv = (h @ w_qkv).reshape(bsz, seq, 3, C_HEADS, C_HEAD_DIM)
    return qkv[:, :, 0], qkv[:, :, 1], qkv[:, :, 2]


def sb_layer_prompt(h, w_qkv, w_o):
    bsz, seq, _ = h.shape
    q, k, v = sb_project(h, w_qkv)
    n_blocks = seq // QBLOCK
    qb = q.reshape(bsz, n_blocks, QBLOCK, C_HEADS, C_HEAD_DIM).swapaxes(0, 1)
    kpos = jnp.arange(seq)

    def block(args):
        q_b, start = args
        z = jnp.einsum('bqhd,bkhd->bhqk', q_b, k).astype(jnp.float32) * C_SCALE
        qpos = start + jnp.arange(QBLOCK)
        a = stick_breaking_weights(z, kpos[None, :] < qpos[:, None])
        return jnp.einsum('bhqk,bkhd->bqhd', a, v)

    o = lax.map(block, (qb, jnp.arange(n_blocks) * QBLOCK))
    o = o.swapaxes(0, 1).reshape(bsz, seq, C_HEADS * C_HEAD_DIM)
    return (o @ w_o).astype(h.dtype), k, v


def sb_layer_sample(h, pool_k, pool_v, layer, page_table, w_qkv, w_o):
    bsz, t_new, _ = h.shape
    q, k, v = sb_project(h, w_qkv)
    z_past = jnp.einsum('bthd,bshd->bhts', q, gather_pages(pool_k, layer, page_table))
    n_past = z_past.shape[-1]
    z = jnp.concatenate([z_past, jnp.einsum('bthd,bshd->bhts', q, k)], axis=-1).astype(jnp.float32) * C_SCALE
    qpos = n_past + jnp.arange(t_new)
    kpos = jnp.arange(n_past + t_new)
    a = stick_breaking_weights(z, kpos[None, :] < qpos[:, None])
    o = (jnp.einsum('bhts,bshd->bthd', a[..., :n_past], gather_pages(pool_v, layer, page_table))
         + jnp.einsum('bhts,bshd->bthd', a[..., n_past:], v))
    o = o.reshape(bsz, t_new, C_HEADS * C_HEAD_DIM)
    return (o @ w_o).astype(h.dtype), k, v


def setup_inputs(seed: int = 0) -> dict:
    key = jax.random.key(seed)
    keys = iter(jax.random.split(key, 40))

    def normal(shape, scale=1.0):
        return jax.random.normal(next(keys), shape, jnp.float32) * scale

    def gain(shape):
        return 1.0 + normal(shape, 0.02)

    n_pages = PAST_LEN // PAGE_SIZE
    n_used = DEC_BATCH * n_pages
    n_pool = (5 * n_used + 3) // 4
    page_table = jax.random.permutation(next(keys), n_pool)[:n_used].reshape(DEC_BATCH, n_pages).astype(jnp.int32)
    a_in = 2 * A_HEADS * A_DK + 2 * A_HEADS * A_DV + 2 * A_HEADS
    b_down = B_Q_LORA + B_KV_LORA + B_ROPE
    f_bias = jnp.linspace(F_BIAS_LO, F_BIAS_HI, A_HEADS, dtype=jnp.float32)
    return {
        'x_prompt': normal((BATCH, SEQ, D_MODEL)),
        'x_sample': normal((DEC_BATCH, DEC_SEQ, D_MODEL)),
        'state_mlstm_C': normal((N_A, DEC_BATCH, A_HEADS, A_DK, A_DV), 0.5),
        'state_mlstm_n': normal((N_A, DEC_BATCH, A_HEADS, A_DK)),
        'state_mlstm_m': normal((N_A, DEC_BATCH, A_HEADS)),
        'cache_mla_latent': normal((N_B, n_pool, PAGE_SIZE, B_KV_LORA)),
        'cache_mla_krope': normal((N_B, n_pool, PAGE_SIZE, B_ROPE)),
        'cache_sb_k': normal((N_C, n_pool, PAGE_SIZE, C_HEADS, C_HEAD_DIM)),
        'cache_sb_v': normal((N_C, n_pool, PAGE_SIZE, C_HEADS, C_HEAD_DIM)),
        'page_table': page_table,
        'g_mix_pre': gain((DEPTH, D_MODEL)),
        'g_mix_post': gain((DEPTH, D_MODEL)),
        'g_ffn_pre': gain((DEPTH, D_MODEL)),
        'g_ffn_post': gain((DEPTH, D_MODEL)),
        'w_ffn_in': normal((DEPTH, D_MODEL, 2 * D_FF), D_MODEL ** -0.5),
        'w_ffn_out': normal((DEPTH, D_FF, D_MODEL), D_FF ** -0.5),
        'w_a_in': normal((N_A, D_MODEL, a_in), D_MODEL ** -0.5),
        'b_a_gate': jnp.concatenate([normal((N_A, A_HEADS), 0.1), f_bias + normal((N_A, A_HEADS), 0.1)], axis=-1),
        'g_a_head': gain((N_A, A_HEADS * A_DV)),
        'w_a_out': normal((N_A, A_HEADS * A_DV, D_MODEL), (A_HEADS * A_DV) ** -0.5),
        'w_b_down': normal((N_B, D_MODEL, b_down), D_MODEL ** -0.5),
        'g_b_q': gain((N_B, B_Q_LORA)),
        'g_b_kv': gain((N_B, B_KV_LORA)),
        'w_b_uq': normal((N_B, B_Q_LORA, B_HEADS * (B_NOPE + B_ROPE)), B_Q_LORA ** -0.5),
        'w_b_uk': normal((N_B, B_KV_LORA, B_HEADS, B_NOPE), B_KV_LORA ** -0.5),
        'w_b_uv': normal((N_B, B_KV_LORA, B_HEADS, B_VDIM), B_KV_LORA ** -0.5),
        'w_b_o': normal((N_B, B_HEADS * B_VDIM, D_MODEL), (B_HEADS * B_VDIM) ** -0.5),
        'w_c_qkv': normal((N_C, D_MODEL, 3 * C_HEADS * C_HEAD_DIM), D_MODEL ** -0.5),
        'w_c_o': normal((N_C, C_HEADS * C_HEAD_DIM, D_MODEL), (C_HEADS * C_HEAD_DIM) ** -0.5),
    }


def reference(x_prompt, x_sample, state_mlstm_C, state_mlstm_n, state_mlstm_m,
              cache_mla_latent, cache_mla_krope, cache_sb_k, cache_sb_v, page_table,
              g_mix_pre, g_mix_post, g_ffn_pre, g_ffn_post, w_ffn_in, w_ffn_out,
              w_a_in, b_a_gate, g_a_head, w_a_out,
              w_b_down, g_b_q, g_b_kv, w_b_uq, w_b_uk, w_b_uv, w_b_o,
              w_c_qkv, w_c_o):
    n_prompt = x_prompt.shape[0]
    pos_prompt = jnp.arange(x_prompt.shape[1], dtype=jnp.int32)
    pos_sample = PAST_LEN + jnp.arange(x_sample.shape[1], dtype=jnp.int32)
    yp, ys = x_prompt, x_sample
    a_prompt, a_sample, b_prompt, b_sample, c_prompt, c_sample = [], [], [], [], [], []
    for i in range(DEPTH):
        kind, j = i % N_MIXERS, i // N_MIXERS
        hp = rms_norm(yp, g_mix_pre[i])
        hs = rms_norm(ys, g_mix_pre[i])
        if kind == 0:
            c0 = jnp.zeros((n_prompt, A_HEADS, A_DK, A_DV), jnp.float32)
            n0 = jnp.zeros((n_prompt, A_HEADS, A_DK), jnp.float32)
            m0 = jnp.full((n_prompt, A_HEADS), M_INIT, jnp.float32)
            mp, *st_p = mlstm_layer(hp, c0, n0, m0, w_a_in[j], b_a_gate[j], g_a_head[j], w_a_out[j],
                                    min(MLSTM_CHUNK, hp.shape[1]))
            ms, *st_s = mlstm_layer(hs, state_mlstm_C[j], state_mlstm_n[j], state_mlstm_m[j],
                                    w_a_in[j], b_a_gate[j], g_a_head[j], w_a_out[j], hs.shape[1])
            a_prompt.append(st_p)
            a_sample.append(st_s)
        elif kind == 1:
            mp, *st_p = mla_layer_prompt(hp, pos_prompt, w_b_down[j], g_b_q[j], g_b_kv[j],
                                         w_b_uq[j], w_b_uk[j], w_b_uv[j], w_b_o[j])
            ms, *st_s = mla_layer_sample(hs, pos_sample, cache_mla_latent, cache_mla_krope, j, page_table,
                                         w_b_down[j], g_b_q[j], g_b_kv[j], w_b_uq[j], w_b_uk[j],
                                         w_b_uv[j], w_b_o[j])
            b_prompt.append(st_p)
            b_sample.append(st_s)
        else:
            mp, *st_p = sb_layer_prompt(hp, w_c_qkv[j], w_c_o[j])
            ms, *st_s = sb_layer_sample(hs, cache_sb_k, cache_sb_v, j, page_table, w_c_qkv[j], w_c_o[j])
            c_prompt.append(st_p)
            c_sample.append(st_s)
        yp = yp + rms_norm(mp, g_mix_post[i])
        ys = ys + rms_norm(ms, g_mix_post[i])
        yp = yp + rms_norm(swiglu(rms_norm(yp, g_ffn_pre[i]), w_ffn_in[i], w_ffn_out[i]), g_ffn_post[i])
        ys = ys + rms_norm(swiglu(rms_norm(ys, g_ffn_pre[i]), w_ffn_in[i], w_ffn_out[i]), g_ffn_post[i])

    def stack(states, idx):
        return jnp.stack([s[idx] for s in states])

    return (yp, ys,
            stack(a_prompt, 0), stack(a_prompt, 1), stack(a_prompt, 2),
            stack(a_sample, 0), stack(a_sample, 1), stack(a_sample, 2),
            stack(b_prompt, 0), stack(b_prompt, 1),
            stack(b_sample, 0), stack(b_sample, 1),
            stack(c_prompt, 0), stack(c_prompt, 1),
            stack(c_sample, 0), stack(c_sample, 1))
```

```python
import functools

import jax
import jax.numpy as jnp
from jax import lax
from jax.experimental import pallas as pl
from jax.experimental.pallas import tpu as pltpu

F32 = jnp.float32
BF16 = jnp.bfloat16

D_MODEL = 1024
N_MIXERS = 3
A_HEADS = 4
A_DK = 128
A_DV = 256
M_INIT = -1e30
B_HEADS = 16
B_NOPE = 64
B_ROPE = 32
B_VDIM = 64
B_Q_LORA = 768
B_KV_LORA = 256
ROPE_THETA = 10000.0
MLA_SCALE = (B_NOPE + B_ROPE) ** -0.5
C_HEADS = 16
C_HEAD_DIM = 64
C_SCALE = C_HEAD_DIM ** -0.5
D_FF = 2816
EPS = 1e-6

LANES = 128
HEAD_PAD = LANES
FFN_CHUNK = 256
VMEM_LIMIT = 56 * 1024 * 1024
ROW_TILE = 512
MLSTM_CHUNK = 256
ATT_TILE = 512
SB_TILE = 256
DEC_PAGES = 8
NEG_INF = float("-inf")


def _params(*sem):
    return pltpu.CompilerParams(dimension_semantics=sem, vmem_limit_bytes=VMEM_LIMIT)


def _rms(x, g):
    return x * lax.rsqrt(jnp.mean(x * x, axis=-1, keepdims=True) + EPS) * g


def _log_sigmoid(x):
    return -(jnp.maximum(-x, 0.0) + jnp.log1p(jnp.exp(-jnp.abs(x))))


def _dot(a, b):
    return jnp.dot(a, b, preferred_element_type=F32)


def _dot_nt(a, b):
    return lax.dot_general(a, b, (((1,), (1,)), ((), ())), preferred_element_type=F32)


def _tile(n, t):
    t = min(n, t)
    assert n % t == 0, (n, t)
    return t


def _norm_linear_kernel(*refs, n_out, has_scale):
    x_ref, g_ref, w_ref = refs[:3]
    pos = 3
    s_ref = None
    if has_scale:
        s_ref = refs[pos]
        pos += 1
    outs = refs[pos:pos + n_out]
    h_ref = refs[pos + n_out]

    @pl.when(pl.program_id(1) == 0)
    def _():
        h_ref[...] = _rms(x_ref[...], g_ref[...]).astype(BF16)

    acc = _dot(h_ref[...], w_ref[...])
    if has_scale:
        acc = acc * s_ref[...]
    for o in outs:
        o[...] = acc.astype(o.dtype)


def norm_linear(x, g, w, out_dtypes, col_scale=None, tn=512):
    m, k = x.shape
    n = w.shape[1]
    tm = _tile(m, ROW_TILE)
    tn = _tile(n, tn)
    in_specs = [pl.BlockSpec((tm, k), lambda i, j: (i, 0)),
                pl.BlockSpec((1, k), lambda i, j: (0, 0)),
                pl.BlockSpec((k, tn), lambda i, j: (0, j))]
    args = [x, g.reshape(1, k), w]
    if col_scale is not None:
        in_specs.append(pl.BlockSpec((1, tn), lambda i, j: (0, j)))
        args.append(col_scale.reshape(1, n))
    outs = pl.pallas_call(
        functools.partial(_norm_linear_kernel, n_out=len(out_dtypes), has_scale=col_scale is not None),
        grid=(m // tm, n // tn),
        in_specs=in_specs,
        out_specs=[pl.BlockSpec((tm, tn), lambda i, j: (i, j)) for _ in out_dtypes],
        out_shape=[jax.ShapeDtypeStruct((m, n), dt) for dt in out_dtypes],
        scratch_shapes=[pltpu.VMEM((tm, k), BF16)],
        compiler_params=_params("parallel", "arbitrary"),
        name="norm_linear",
    )(*args)
    return outs


def _linear_norm_res_kernel(y_ref, w_ref, g_ref, r_ref, o_ref):
    acc = _dot(y_ref[...].astype(BF16), w_ref[...])
    o_ref[...] = r_ref[...] + _rms(acc, g_ref[...])


def linear_norm_res(y, w, g, res):
    m, k = y.shape
    n = w.shape[1]
    tm = _tile(m, ROW_TILE)
    return pl.pallas_call(
        _linear_norm_res_kernel,
        grid=(m // tm,),
        in_specs=[pl.BlockSpec((tm, k), lambda i: (i, 0)),
                  pl.BlockSpec((k, n), lambda i: (0, 0)),
                  pl.BlockSpec((1, n), lambda i: (0, 0)),
                  pl.BlockSpec((tm, n), lambda i: (i, 0))],
        out_specs=pl.BlockSpec((tm, n), lambda i: (i, 0)),
        out_shape=jax.ShapeDtypeStruct((m, n), F32),
        compiler_params=_params("parallel"),
        name="linear_norm_res",
    )(y, w, g.reshape(1, n), res)


def _ffn_kernel(x_ref, gp_ref, wg_ref, wu_ref, wo_ref, gq_ref, o_ref):
    x = x_ref[...]
    h = _rms(x, gp_ref[...]).astype(BF16)
    acc = jnp.zeros(x.shape, F32)
    for c in range(D_FF // FFN_CHUNK):
        sl = slice(c * FFN_CHUNK, (c + 1) * FFN_CHUNK)
        a = _dot(h, wg_ref[:, sl])
        u = _dot(h, wu_ref[:, sl])
        act = (a * jax.nn.sigmoid(a) * u).astype(BF16)
        acc = acc + _dot(act, wo_ref[sl, :])
    o_ref[...] = x + _rms(acc, gq_ref[...])


def ffn(x, g_pre, w_gate, w_up, w_out, g_post):
    m, d = x.shape
    tm = _tile(m, ROW_TILE)
    const = lambda i: (0, 0)
    return pl.pallas_call(
        _ffn_kernel,
        grid=(m // tm,),
        in_specs=[pl.BlockSpec((tm, d), lambda i: (i, 0)),
                  pl.BlockSpec((1, d), const),
                  pl.BlockSpec((d, D_FF), const, pipeline_mode=pl.Buffered(1)),
                  pl.BlockSpec((d, D_FF), const, pipeline_mode=pl.Buffered(1)),
                  pl.BlockSpec((D_FF, d), const, pipeline_mode=pl.Buffered(1)),
                  pl.BlockSpec((1, d), const)],
        out_specs=pl.BlockSpec((tm, d), lambda i: (i, 0)),
        out_shape=jax.ShapeDtypeStruct((m, d), F32),
        compiler_params=_params("parallel"),
        name="ffn",
    )(x, g_pre.reshape(1, d), w_gate, w_up, w_out, g_post.reshape(1, d))


def _mlstm_prompt_kernel(q_ref, k_ref, v_ref, o_ref, gt_ref, bg_ref, gh_ref,
                         y_ref, c_ref, n_ref, m_ref, *, chunk):
    L = chunk

    @pl.when(pl.program_id(1) == 0)
    def _():
        c_ref[...] = jnp.zeros(c_ref.shape, F32)
        n_ref[...] = jnp.zeros(n_ref.shape, F32)
        m_ref[...] = jnp.full(m_ref.shape, M_INIT, F32)

    gates = gt_ref[...] + bg_ref[...]
    lane = lax.broadcasted_iota(jnp.int32, (L, LANES), 1)
    log_f = jnp.where(lane >= A_HEADS, _log_sigmoid(gates), 0.0)
    row = lax.broadcasted_iota(jnp.int32, (L, L), 0)
    col = lax.broadcasted_iota(jnp.int32, (L, L), 1)
    causal = col <= row
    tri = jnp.where(causal, 1.0, 0.0).astype(F32)
    b_all = jnp.dot(tri, log_f, precision=lax.Precision.HIGHEST, preferred_element_type=F32)
    packed = jnp.where(lane < A_HEADS, gates, b_all)
    packed_t = packed.T

    for h in range(A_HEADS):
        li_col = packed[:, h:h + 1]
        b_col = packed[:, A_HEADS + h:A_HEADS + h + 1]
        li_row = packed_t[h:h + 1, :]
        b_row = packed_t[A_HEADS + h:A_HEADS + h + 1, :]
        q = q_ref[:, h * A_DK:(h + 1) * A_DK]
        k = k_ref[:, h * A_DK:(h + 1) * A_DK]
        v = v_ref[:, h * A_DV:(h + 1) * A_DV]
        m_prev = m_ref[0, :, h:h + 1]
        c_mem = c_ref[0, h]
        n_row = n_ref[0, h:h + 1, :]

        log_d = jnp.where(causal, b_col - b_row + li_row, NEG_INF)
        log_inter = b_col + m_prev
        m_t = jnp.maximum(log_inter, jnp.max(log_d, axis=1, keepdims=True))
        s = _dot_nt(q, k) * jnp.exp(log_d - m_t)
        w_inter = jnp.exp(log_inter - m_t)
        num = w_inter * _dot(q, c_mem.astype(BF16)) + _dot(s.astype(BF16), v)
        qn = jnp.sum(q.astype(F32) * n_row, axis=1, keepdims=True)
        den = w_inter * qn + jnp.sum(s, axis=1, keepdims=True)
        hh = num / jnp.maximum(jnp.abs(den), jnp.exp(-m_t))
        hn = _rms(hh, gh_ref[:, h * A_DV:(h + 1) * A_DV])
        gate = jax.nn.sigmoid(o_ref[:, h * A_DV:(h + 1) * A_DV])
        y_ref[:, h * A_DV:(h + 1) * A_DV] = (gate * hn).astype(y_ref.dtype)

        m_new = m_t[L - 1:L, :]
        b_last = b_col[L - 1:L, :]
        w_col = jnp.exp(b_last - b_col + li_col - m_new)
        decay = jnp.exp(b_last + m_prev - m_new)
        kw = k.astype(F32) * w_col
        c_ref[0, h] = decay * c_mem + _dot(kw.T.astype(BF16), v)
        n_ref[0, h:h + 1, :] = decay * n_row + jnp.sum(kw, axis=0, keepdims=True)
        m_ref[0, :, h:h + 1] = m_new


def mlstm_prompt(qkv, o_gate, gates, b_gate, g_head, batch, seq):
    L = _tile(seq, MLSTM_CHUNK)
    nc = seq // L
    hk = A_HEADS * A_DK
    hv = A_HEADS * A_DV
    m = batch * seq
    row_map = lambda b, c: (b * nc + c, 0)
    const = lambda b, c: (0, 0)
    bg = jnp.zeros((1, LANES), F32).at[0, :2 * A_HEADS].set(b_gate)
    y, c_fin, n_fin, m_fin = pl.pallas_call(
        functools.partial(_mlstm_prompt_kernel, chunk=L),
        grid=(batch, nc),
        in_specs=[pl.BlockSpec((L, hk), row_map),
                  pl.BlockSpec((L, hk), lambda b, c: (b * nc + c, 1)),
                  pl.BlockSpec((L, hv), lambda b, c: (b * nc + c, 1)),
                  pl.BlockSpec((L, hv), row_map),
                  pl.BlockSpec((L, LANES), row_map),
                  pl.BlockSpec((1, LANES), const),
                  pl.BlockSpec((1, hv), const)],
        out_specs=[pl.BlockSpec((L, hv), row_map),
                   pl.BlockSpec((1, A_HEADS, A_DK, A_DV), lambda b, c: (b, 0, 0, 0)),
                   pl.BlockSpec((1, A_HEADS, A_DK), lambda b, c: (b, 0, 0)),
                   pl.BlockSpec((1, 1, LANES), lambda b, c: (b, 0, 0))],
        out_shape=[jax.ShapeDtypeStruct((m, hv), BF16),
                   jax.ShapeDtypeStruct((batch, A_HEADS, A_DK, A_DV), F32),
                   jax.ShapeDtypeStruct((batch, A_HEADS, A_DK), F32),
                   jax.ShapeDtypeStruct((batch, 1, LANES), F32)],
        compiler_params=_params("parallel", "arbitrary"),
        name="mlstm_prompt",
    )(qkv, qkv, qkv, o_gate, gates, bg, g_head.reshape(1, hv))
    return y, c_fin, n_fin, m_fin[:, 0, :A_HEADS]


def _mlstm_decode_kernel(q_ref, k_ref, v_ref, o_ref, gt_ref, bg_ref, gh_ref, c_ref, n_ref, m_ref,
                         y_ref, co_ref, no_ref, mo_ref, *, tb):
    r = lax.broadcasted_iota(jnp.int32, (A_DK, A_DK), 0)
    c = lax.broadcasted_iota(jnp.int32, (A_DK, A_DK), 1)
    eye = r == c
    lane = lax.broadcasted_iota(jnp.int32, (1, LANES), 1)

    def to_col(x_row):
        return jnp.sum(jnp.where(eye, jnp.broadcast_to(x_row, (A_DK, A_DK)), 0.0), axis=1, keepdims=True)

    def body(bb, carry):
        gates = gt_ref[bb] + bg_ref[...]
        log_f = _log_sigmoid(gates)
        m_row = m_ref[bb]
        m_out = jnp.zeros((1, LANES), F32)
        q_all = q_ref[bb]
        k_all = k_ref[bb]
        v_all = v_ref[bb]
        o_all = o_ref[bb]
        for h in range(A_HEADS):
            li = gates[:, h:h + 1]
            lf = log_f[:, A_HEADS + h:A_HEADS + h + 1]
            m_prev = m_row[:, h:h + 1]
            m_t = jnp.maximum(lf + m_prev, li)
            w_inter = jnp.exp(lf + m_prev - m_t)
            w_new = jnp.exp(li - m_t)
            q = q_all[:, h * A_DK:(h + 1) * A_DK]
            k = k_all[:, h * A_DK:(h + 1) * A_DK]
            v = v_all[:, h * A_DV:(h + 1) * A_DV]
            c_mem = c_ref[bb, h]
            n_row = n_ref[bb, pl.ds(h, 1), :]
            s = jnp.sum(q * k, axis=1, keepdims=True) * w_new
            qc = jnp.sum(c_mem * to_col(q), axis=0, keepdims=True)
            num = w_inter * qc + s * v
            den = w_inter * jnp.sum(q * n_row, axis=1, keepdims=True) + s
            hh = num / jnp.maximum(jnp.abs(den), jnp.exp(-m_t))
            hn = _rms(hh, gh_ref[:, h * A_DV:(h + 1) * A_DV])
            gate = jax.nn.sigmoid(o_all[:, h * A_DV:(h + 1) * A_DV])
            y_ref[bb, :, h * A_DV:(h + 1) * A_DV] = gate * hn
            kw = k * w_new
            co_ref[bb, h] = w_inter * c_mem + to_col(kw) * v
            no_ref[bb, pl.ds(h, 1), :] = w_inter * n_row + kw
            m_out = jnp.where(lane == h, m_t, m_out)
        mo_ref[bb] = m_out
        return carry

    lax.fori_loop(0, tb, body, 0)


def mlstm_decode(q, k, v, o_gate, gates, b_gate, g_head, c0, n0, m0):
    nb = q.shape[0]
    tb = _tile(nb, 8)
    hk = A_HEADS * A_DK
    hv = A_HEADS * A_DV
    bg = jnp.zeros((1, LANES), F32).at[0, :2 * A_HEADS].set(b_gate)
    m_pad = jnp.zeros((nb, 1, LANES), F32).at[:, 0, :A_HEADS].set(m0)
    row3 = lambda i: (i, 0, 0)
    const = lambda i: (0, 0)
    y, c1, n1, m1 = pl.pallas_call(
        functools.partial(_mlstm_decode_kernel, tb=tb),
        grid=(nb // tb,),
        in_specs=[pl.BlockSpec((tb, 1, hk), row3),
                  pl.BlockSpec((tb, 1, hk), row3),
                  pl.BlockSpec((tb, 1, hv), row3),
                  pl.BlockSpec((tb, 1, hv), row3),
                  pl.BlockSpec((tb, 1, LANES), row3),
                  pl.BlockSpec((1, LANES), const),
                  pl.BlockSpec((1, hv), const),
                  pl.BlockSpec((tb, A_HEADS, A_DK, A_DV), lambda i: (i, 0, 0, 0)),
                  pl.BlockSpec((tb, A_HEADS, A_DK), row3),
                  pl.BlockSpec((tb, 1, LANES), row3)],
        out_specs=[pl.BlockSpec((tb, 1, hv), row3),
                   pl.BlockSpec((tb, A_HEADS, A_DK, A_DV), lambda i: (i, 0, 0, 0)),
                   pl.BlockSpec((tb, A_HEADS, A_DK), row3),
                   pl.BlockSpec((tb, 1, LANES), row3)],
        out_shape=[jax.ShapeDtypeStruct((nb, 1, hv), F32),
                   jax.ShapeDtypeStruct((nb, A_HEADS, A_DK, A_DV), F32),
                   jax.ShapeDtypeStruct((nb, A_HEADS, A_DK), F32),
                   jax.ShapeDtypeStruct((nb, 1, LANES), F32)],
        compiler_params=_params("parallel"),
        name="mlstm_decode",
    )(q.reshape(nb, 1, hk), k.reshape(nb, 1, hk), v.reshape(nb, 1, hv), o_gate.reshape(nb, 1, hv),
      gates.reshape(nb, 1, LANES), bg, g_head.reshape(1, hv), c0, n0, m_pad)
    return y.reshape(nb, hv), c1, n1, m1[:, 0, :A_HEADS]


def _rope_tables(pos):
    half = B_ROPE // 2
    inv_freq = ROPE_THETA ** (-jnp.arange(half, dtype=F32) / half)
    ang = pos.astype(F32)[:, None] * inv_freq[None, :]
    cos, sin = jnp.cos(ang), jnp.sin(ang)
    cos_k = jnp.concatenate([cos, cos], axis=1)
    sin_k = jnp.concatenate([-sin, sin], axis=1)
    ones = jnp.ones((pos.shape[0], B_NOPE), F32)
    tail = HEAD_PAD - B_NOPE - B_ROPE
    cos_q = jnp.concatenate([ones, cos_k, jnp.ones((pos.shape[0], tail), F32)], axis=1)
    sin_q = jnp.concatenate([0 * ones, sin_k, jnp.zeros((pos.shape[0], tail), F32)], axis=1)
    return cos_k, sin_k, cos_q, sin_q


def _mla_proj_kernel(*refs, want_kv):
    (x_ref, gp_ref, wd_ref, gq_ref, gkv_ref, ck_ref, sk_ref, cq_ref, sq_ref, wuq_ref) = refs[:10]
    if want_kv:
        wk_ref, we_ref, wv_ref = refs[10:13]
        ckv_ref, kr_ref, q_ref, kf_ref, v_ref = refs[13:]
    else:
        ckv_ref, kr_ref, q_ref = refs[10:]
    half = B_ROPE // 2
    h = _rms(x_ref[...], gp_ref[...]).astype(BF16)
    d = _dot(h, wd_ref[...])
    c_q = _rms(d[:, :B_Q_LORA], gq_ref[...]).astype(BF16)
    c_kv = _rms(d[:, B_Q_LORA:B_Q_LORA + B_KV_LORA], gkv_ref[...])
    ckv_ref[...] = c_kv
    kr = d[:, B_Q_LORA + B_KV_LORA:B_Q_LORA + B_KV_LORA + B_ROPE]
    kr_rot = jnp.concatenate([kr[:, half:], kr[:, :half]], axis=1)
    k_rope = kr * ck_ref[...] + kr_rot * sk_ref[...]
    kr_ref[...] = k_rope

    q = _dot(c_q, wuq_ref[...])
    lane = lax.broadcasted_iota(jnp.int32, (q.shape[0], HEAD_PAD), 1)
    first = lane < B_NOPE + half
    cq, sq = cq_ref[...], sq_ref[...]
    for hd in range(B_HEADS):
        xg = q[:, hd * HEAD_PAD:(hd + 1) * HEAD_PAD]
        rot = jnp.where(first, pltpu.roll(xg, HEAD_PAD - half, 1), pltpu.roll(xg, half, 1))
        q_ref[:, hd * HEAD_PAD:(hd + 1) * HEAD_PAD] = (xg * cq + rot * sq).astype(q_ref.dtype)

    if want_kv:
        c_kv_b = c_kv.astype(BF16)
        kr_pad = jnp.concatenate([k_rope, jnp.zeros((k_rope.shape[0], LANES - B_ROPE), F32)], axis=1).astype(BF16)
        kf_ref[...] = (_dot(c_kv_b, wk_ref[...]) + _dot(kr_pad, we_ref[...])).astype(BF16)
        v_ref[...] = _dot(c_kv_b, wv_ref[...]).astype(BF16)


def mla_proj(x, g_pre, w, tabs, n_pos_blocks, want_kv, q_dtype):
    m, d = x.shape
    tm = _tile(m, ROW_TILE)
    hp = B_HEADS * HEAD_PAD
    cos_k, sin_k, cos_q, sin_q = tabs
    assert cos_k.shape[0] == n_pos_blocks * tm
    const = lambda i: (0, 0)
    row = lambda i: (i, 0)
    posm = lambda i: (i % n_pos_blocks, 0)
    in_specs = [pl.BlockSpec((tm, d), row),
                pl.BlockSpec((1, d), const),
                pl.BlockSpec(w["down"].shape, const),
                pl.BlockSpec((1, B_Q_LORA), const),
                pl.BlockSpec((1, B_KV_LORA), const),
                pl.BlockSpec((tm, B_ROPE), posm),
                pl.BlockSpec((tm, B_ROPE), posm),
                pl.BlockSpec((tm, HEAD_PAD), posm),
                pl.BlockSpec((tm, HEAD_PAD), posm),
                pl.BlockSpec(w["uq"].shape, const)]
    args = [x, g_pre.reshape(1, d), w["down"], w["g_q"], w["g_kv"], cos_k, sin_k, cos_q, sin_q, w["uq"]]
    out_specs = [pl.BlockSpec((tm, B_KV_LORA), row),
                 pl.BlockSpec((tm, B_ROPE), row),
                 pl.BlockSpec((tm, hp), row)]
    out_shape = [jax.ShapeDtypeStruct((m, B_KV_LORA), F32),
                 jax.ShapeDtypeStruct((m, B_ROPE), F32),
                 jax.ShapeDtypeStruct((m, hp), q_dtype)]
    if want_kv:
        in_specs += [pl.BlockSpec(w["uk"].shape, const),
                     pl.BlockSpec(w["rope_place"].shape, const),
                     pl.BlockSpec(w["uv"].shape, const)]
        args += [w["uk"], w["rope_place"], w["uv"]]
        out_specs += [pl.BlockSpec((tm, hp), row), pl.BlockSpec((tm, hp), row)]
        out_shape += [jax.ShapeDtypeStruct((m, hp), BF16), jax.ShapeDtypeStruct((m, hp), BF16)]
    return pl.pallas_call(
        functools.partial(_mla_proj_kernel, want_kv=want_kv),
        grid=(m // tm,),
        in_specs=in_specs, out_specs=out_specs, out_shape=out_shape,
        compiler_params=_params("parallel"),
        name="mla_proj",
    )(*args)


def _mla_attn_kernel(q_ref, k_ref, v_ref, o_ref, m_scr, l_scr, acc_scr, *, t):
    i = pl.program_id(2)
    q = q_ref[...]
    m_scr[...] = jnp.full(m_scr.shape, NEG_INF, F32)
    l_scr[...] = jnp.zeros(l_scr.shape, F32)
    acc_scr[...] = jnp.zeros(acc_scr.shape, F32)
    row = lax.broadcasted_iota(jnp.int32, (t, t), 0)
    col = lax.broadcasted_iota(jnp.int32, (t, t), 1)

    def step(j, masked):
        start = pl.multiple_of(j * t, t)
        kb = k_ref[pl.ds(start, t), :]
        vb = v_ref[pl.ds(start, t), :]
        s = _dot_nt(q, kb) * MLA_SCALE
        if masked:
            s = jnp.where(col <= row, s, NEG_INF)
        m_old = m_scr[...]
        m_new = jnp.maximum(m_old, jnp.max(s, axis=1, keepdims=True))
        alpha = jnp.exp(m_old - m_new)
        p = jnp.exp(s - m_new)
        l_scr[...] = alpha * l_scr[...] + jnp.sum(p, axis=1, keepdims=True)
        acc_scr[...] = alpha * acc_scr[...] + _dot(p.astype(BF16), vb)
        m_scr[...] = m_new

    def body(j, carry):
        step(j, False)
        return carry

    lax.fori_loop(0, i, body, 0)
    step(i, True)
    o_ref[...] = (acc_scr[...] / l_scr[...]).astype(o_ref.dtype)


def mla_attn(q, kf, v, batch, seq):
    t = _tile(seq, ATT_TILE)
    nq = seq // t
    m = batch * seq
    hp = B_HEADS * HEAD_PAD
    return pl.pallas_call(
        functools.partial(_mla_attn_kernel, t=t),
        grid=(batch, B_HEADS, nq),
        in_specs=[pl.BlockSpec((t, HEAD_PAD), lambda b, h, i: (b * nq + i, h)),
                  pl.BlockSpec((seq, HEAD_PAD), lambda b, h, i: (b, h)),
                  pl.BlockSpec((seq, HEAD_PAD), lambda b, h, i: (b, h))],
        out_specs=pl.BlockSpec((t, HEAD_PAD), lambda b, h, i: (b * nq + i, h)),
        out_shape=jax.ShapeDtypeStruct((m, hp), BF16),
        scratch_shapes=[pltpu.VMEM((t, 1), F32), pltpu.VMEM((t, 1), F32), pltpu.VMEM((t, HEAD_PAD), F32)],
        compiler_params=_params("parallel", "parallel", "arbitrary"),
        name="mla_attn",
    )(q, kf, v)


def _head_matmul_kernel(x_ref, w_ref, o_ref):
    o_ref[...] = _dot(x_ref[...].astype(BF16), w_ref[0]).astype(o_ref.dtype)


def head_matmul(x, w, out_dtype):
    nb = x.shape[0]
    nh, k, n = w.shape
    return pl.pallas_call(
        _head_matmul_kernel,
        grid=(nh,),
        in_specs=[pl.BlockSpec((nb, k), lambda h: (0, h)),
                  pl.BlockSpec((1, k, n), lambda h: (h, 0, 0))],
        out_specs=pl.BlockSpec((nb, n), lambda h: (0, h)),
        out_shape=jax.ShapeDtypeStruct((nb, nh * n), out_dtype),
        compiler_params=_params("parallel"),
        name="head_matmul",
    )(x, w)


def _mla_decode_kernel(pt_ref, ql_ref, qr_ref, cn_ref, kn_ref, *refs, n_pages):
    lat_refs = refs[:n_pages]
    kr_refs = refs[n_pages:2 * n_pages]
    o_ref, m_scr, l_scr, acc_scr = refs[2 * n_pages:]
    s_id = pl.program_id(1)
    ql = ql_ref[0]
    qr = qr_ref[0]

    @pl.when(s_id == 0)
    def _():
        cn = cn_ref[0]
        kn = kn_ref[0]
        s_new = (jnp.sum(ql.astype(F32) * cn, axis=1, keepdims=True)
                 + jnp.sum(qr.astype(F32) * kn, axis=1, keepdims=True)) * MLA_SCALE
        m_scr[...] = s_new
        l_scr[...] = jnp.ones(l_scr.shape, F32)
        acc_scr[...] = jnp.broadcast_to(cn, acc_scr.shape)

    for g in range(n_pages):
        lat = lat_refs[g][...].astype(BF16)
        kr = kr_refs[g][...].astype(BF16)
        s = (_dot_nt(ql, lat) + _dot(qr, kr)) * MLA_SCALE
        m_old = m_scr[...]
        m_new = jnp.maximum(m_old, jnp.max(s, axis=1, keepdims=True))
        alpha = jnp.exp(m_old - m_new)
        p = jnp.exp(s - m_new)
        l_scr[...] = alpha * l_scr[...] + jnp.sum(p, axis=1, keepdims=True)
        acc_scr[...] = alpha * acc_scr[...] + _dot(p.astype(BF16), lat)
        m_scr[...] = m_new

    @pl.when(s_id == pl.num_programs(1) - 1)
    def _():
        o_ref[0] = acc_scr[...] / l_scr[...]


def mla_decode(q_lat, q_rope, c_new, k_new, pool_lat, pool_kr, layer, page_table):
    nb, n_pages = page_table.shape
    page = pool_lat.shape[2]
    g = _tile(n_pages, DEC_PAGES)
    steps = n_pages // g

    def page_map(gi):
        return lambda b, s, pt: (layer, pt[b * n_pages + s * g + gi], 0, 0)

    bmap = lambda b, s, pt: (b, 0, 0)
    in_specs = [pl.BlockSpec((1, B_HEADS, B_KV_LORA), bmap),
                pl.BlockSpec((1, B_HEADS, B_ROPE), bmap),
                pl.BlockSpec((1, 1, B_KV_LORA), bmap),
                pl.BlockSpec((1, 1, B_ROPE), bmap)]
    in_specs += [pl.BlockSpec((None, None, page, B_KV_LORA), page_map(gi)) for gi in range(g)]
    in_specs += [pl.BlockSpec((None, None, B_ROPE, page), page_map(gi)) for gi in range(g)]
    grid_spec = pltpu.PrefetchScalarGridSpec(
        num_scalar_prefetch=1,
        grid=(nb, steps),
        in_specs=in_specs,
        out_specs=pl.BlockSpec((1, B_HEADS, B_KV_LORA), bmap),
        scratch_shapes=[pltpu.VMEM((B_HEADS, 1), F32), pltpu.VMEM((B_HEADS, 1), F32),
                        pltpu.VMEM((B_HEADS, B_KV_LORA), F32)])
    return pl.pallas_call(
        functools.partial(_mla_decode_kernel, n_pages=g),
        grid_spec=grid_spec,
        out_shape=jax.ShapeDtypeStruct((nb, B_HEADS, B_KV_LORA), F32),
        compiler_params=_params("parallel", "arbitrary"),
        name="mla_decode",
    )(page_table.reshape(-1), q_lat, q_rope, c_new.reshape(nb, 1, B_KV_LORA), k_new.reshape(nb, 1, B_ROPE),
      *([pool_lat] * g), *([jnp.transpose(pool_kr, (0, 1, 3, 2))] * g))


def _suffix_sums(log_rest, ones_tri):
    hi = log_rest.astype(BF16)
    lo = (log_rest - hi.astype(F32)).astype(BF16)
    return _dot(hi, ones_tri) + _dot(lo, ones_tri)


def _sb_attn_kernel(q_ref, k_ref, v_ref, o_ref, acc_scr, c_scr, *, t):
    i = pl.program_id(2)
    half = C_HEAD_DIM
    lane = lax.broadcasted_iota(jnp.int32, (t, 2 * half), 1)
    qs = q_ref[...] * jnp.asarray(C_SCALE, BF16)
    zero = jnp.zeros_like(qs)
    q_heads = (jnp.where(lane < half, qs, zero), jnp.where(lane >= half, qs, zero))
    row = lax.broadcasted_iota(jnp.int32, (t, t), 0)
    col = lax.broadcasted_iota(jnp.int32, (t, t), 1)
    ones_tri = jnp.where(row >= col, 1.0, 0.0).astype(BF16)
    valid = col < row
    acc_scr[...] = jnp.zeros(acc_scr.shape, F32)
    c_scr[...] = jnp.zeros(c_scr.shape, F32)

    def step(j, masked):
        start = pl.multiple_of(j * t, t)
        kb = k_ref[pl.ds(start, t), :]
        vb = v_ref[pl.ds(start, t), :]
        for hh in range(2):
            z = _dot_nt(q_heads[hh], kb)
            log_rest = _log_sigmoid(-z)
            if masked:
                log_rest = jnp.where(valid, log_rest, 0.0)
            tail = _suffix_sums(log_rest, ones_tri)
            a = jnp.exp(z + tail + c_scr[hh])
            if masked:
                a = jnp.where(valid, a, 0.0)
            acc_scr[hh] = acc_scr[hh] + _dot(a.astype(BF16), vb)
            c_scr[hh] = c_scr[hh] + tail[:, 0:1]

    step(i, True)

    def body(r, carry):
        step(i - 1 - r, False)
        return carry

    lax.fori_loop(0, i, body, 0)
    o_ref[...] = jnp.where(lane < half, acc_scr[0], acc_scr[1]).astype(o_ref.dtype)


def sb_attn(q, k, v, batch, seq):
    t = _tile(seq, SB_TILE)
    nq = seq // t
    m = batch * seq
    w = 2 * C_HEAD_DIM
    return pl.pallas_call(
        functools.partial(_sb_attn_kernel, t=t),
        grid=(batch, C_HEADS // 2, nq),
        in_specs=[pl.BlockSpec((t, w), lambda b, h, i: (b * nq + i, h)),
                  pl.BlockSpec((seq, w), lambda b, h, i: (b, h)),
                  pl.BlockSpec((seq, w), lambda b, h, i: (b, h))],
        out_specs=pl.BlockSpec((t, w), lambda b, h, i: (b * nq + i, h)),
        out_shape=jax.ShapeDtypeStruct((m, C_HEADS * C_HEAD_DIM), BF16),
        scratch_shapes=[pltpu.VMEM((2, t, w), F32), pltpu.VMEM((2, t, 1), F32)],
        compiler_params=_params("parallel", "parallel", "arbitrary"),
        name="sb_attn",
    )(q, k, v)


def _sb_decode_kernel(pt_ref, q_ref, kn_ref, vn_ref, *refs, n_pages, new_valid):
    k_refs = refs[:n_pages]
    v_refs = refs[n_pages:2 * n_pages]
    o_ref, c_scr, acc_scr = refs[2 * n_pages:]
    s_id = pl.program_id(1)
    hd = C_HEADS * C_HEAD_DIM
    page = k_refs[0].shape[1]
    head_of_lane = lax.broadcasted_iota(jnp.int32, (C_HEADS, hd), 1) // C_HEAD_DIM
    own = head_of_lane == lax.broadcasted_iota(jnp.int32, (C_HEADS, hd), 0)
    q_row = q_ref[0] * C_SCALE
    q_bd = jnp.where(own, jnp.broadcast_to(q_row, (C_HEADS, hd)), 0.0).astype(BF16)
    row = lax.broadcasted_iota(jnp.int32, (page, page), 0)
    col = lax.broadcasted_iota(jnp.int32, (page, page), 1)
    ones_tri = jnp.where(row >= col, 1.0, 0.0).astype(BF16)

    @pl.when(s_id == 0)
    def _():
        z_new = jnp.sum(jnp.where(own, q_row * kn_ref[0], 0.0), axis=1, keepdims=True)
        a_new = jnp.where(new_valid, jnp.exp(_log_sigmoid(z_new)), 0.0)
        c_scr[...] = jnp.where(new_valid, _log_sigmoid(-z_new), 0.0)
        acc_scr[...] = a_new * jnp.broadcast_to(vn_ref[0], acc_scr.shape)

    for g in range(n_pages):
        kt = k_refs[g][...].astype(BF16)
        vt = v_refs[g][...].astype(BF16)
        z = _dot(q_bd, kt)
        tail = _suffix_sums(_log_sigmoid(-z), ones_tri)
        a = jnp.exp(z + tail + c_scr[...])
        acc_scr[...] = acc_scr[...] + _dot_nt(a.astype(BF16), vt)
        c_scr[...] = c_scr[...] + tail[:, 0:1]

    @pl.when(s_id == pl.num_programs(1) - 1)
    def _():
        o_ref[0] = jnp.sum(jnp.where(own, acc_scr[...], 0.0), axis=0, keepdims=True)


def sb_decode(q, k_new, v_new, pool_k, pool_v, layer, page_table):
    nb, n_pages = page_table.shape
    page = pool_k.shape[2]
    hd = C_HEADS * C_HEAD_DIM
    g = _tile(n_pages, DEC_PAGES)
    steps = n_pages // g
    pk = jnp.transpose(pool_k, (0, 1, 3, 4, 2)).reshape(pool_k.shape[0], pool_k.shape[1], hd, page)
    pv = jnp.transpose(pool_v, (0, 1, 3, 4, 2)).reshape(pool_v.shape[0], pool_v.shape[1], hd, page)
    past_len = n_pages * page
    new_valid = past_len < past_len

    def page_map(gi):
        return lambda b, s, pt: (layer, pt[b * n_pages + n_pages - 1 - (s * g + gi)], 0, 0)

    bmap = lambda b, s, pt: (b, 0, 0)
    in_specs = [pl.BlockSpec((1, 1, hd), bmap)] * 3
    in_specs += [pl.BlockSpec((None, None, hd, page), page_map(gi)) for gi in range(g)] * 2
    grid_spec = pltpu.PrefetchScalarGridSpec(
        num_scalar_prefetch=1,
        grid=(nb, steps),
        in_specs=in_specs,
        out_specs=pl.BlockSpec((1, 1, hd), bmap),
        scratch_shapes=[pltpu.VMEM((C_HEADS, 1), F32), pltpu.VMEM((C_HEADS, hd), F32)])
    o = pl.pallas_call(
        functools.partial(_sb_decode_kernel, n_pages=g, new_valid=new_valid),
        grid_spec=grid_spec,
        out_shape=jax.ShapeDtypeStruct((nb, 1, hd), F32),
        compiler_params=_params("parallel", "arbitrary"),
        name="sb_decode",
    )(page_table.reshape(-1), q.reshape(nb, 1, hd), k_new.reshape(nb, 1, hd), v_new.reshape(nb, 1, hd),
      *([pk] * g), *([pv] * g))
    return o.reshape(nb, hd)


def _pad_heads(w, n_heads, width):
    k = w.shape[0]
    w = w.reshape(k, n_heads, width)
    return jnp.pad(w, ((0, 0), (0, 0), (0, HEAD_PAD - width))).reshape(k, n_heads * HEAD_PAD)


def _mla_weights(w_down, g_q, g_kv, w_uq, w_uk, w_uv, w_o):
    d_down = B_Q_LORA + B_KV_LORA + B_ROPE
    down_pad = -d_down % LANES
    place = jnp.zeros((LANES, B_HEADS, HEAD_PAD), F32)
    idx = jnp.arange(B_ROPE)
    place = place.at[idx, :, B_NOPE + idx].set(1.0).reshape(LANES, B_HEADS * HEAD_PAD)
    uk2 = w_uk.reshape(B_KV_LORA, B_HEADS * B_NOPE)
    uv2 = w_uv.reshape(B_KV_LORA, B_HEADS * B_VDIM)
    uv_pad = _pad_heads(uv2, B_HEADS, B_VDIM).astype(BF16)
    wo_pad = jnp.pad(w_o.reshape(B_HEADS, B_VDIM, D_MODEL), ((0, 0), (0, HEAD_PAD - B_VDIM), (0, 0)))
    return {
        "down": jnp.pad(w_down, ((0, 0), (0, down_pad))).astype(BF16),
        "g_q": g_q.reshape(1, B_Q_LORA), "g_kv": g_kv.reshape(1, B_KV_LORA),
        "uq": _pad_heads(w_uq, B_HEADS, B_NOPE + B_ROPE).astype(BF16),
        "uk": _pad_heads(uk2, B_HEADS, B_NOPE).astype(BF16),
        "rope_place": place.astype(BF16),
        "uv": uv_pad,
        "uk_t": jnp.pad(jnp.transpose(w_uk, (1, 2, 0)), ((0, 0), (0, HEAD_PAD - B_NOPE), (0, 0))).astype(BF16),
        "uv_h": jnp.transpose(uv_pad.reshape(B_KV_LORA, B_HEADS, HEAD_PAD), (1, 0, 2)),
        "o": wo_pad.reshape(B_HEADS * HEAD_PAD, D_MODEL).astype(BF16),
    }


def _mlstm_layer(xp, xs, g_pre, g_post, w_in, b_gate, g_head, w_out, state, batch, seq):
    hk, hv = A_HEADS * A_DK, A_HEADS * A_DV
    w_qkv = w_in[:, :2 * hk + hv].astype(BF16)
    w_o = w_in[:, 2 * hk + hv:2 * hk + 2 * hv].astype(BF16)
    w_g = jnp.pad(w_in[:, 2 * hk + 2 * hv:], ((0, 0), (0, LANES - 2 * A_HEADS))).astype(BF16)
    scale = jnp.concatenate([jnp.ones((hk,), F32), jnp.full((hk,), A_DK ** -0.5, F32), jnp.ones((hv,), F32)])
    w_out_b = w_out.astype(BF16)

    (qkv,) = norm_linear(xp, g_pre, w_qkv, (BF16,), col_scale=scale)
    (og,) = norm_linear(xp, g_pre, w_o, (F32,))
    (gt,) = norm_linear(xp, g_pre, w_g, (F32,))
    y, c_p, n_p, m_p = mlstm_prompt(qkv, og, gt, b_gate, g_head, batch, seq)
    xp = linear_norm_res(y, w_out_b, g_post, xp)

    (qkv_s,) = norm_linear(xs, g_pre, w_qkv, (F32,), col_scale=scale)
    (og_s,) = norm_linear(xs, g_pre, w_o, (F32,))
    (gt_s,) = norm_linear(xs, g_pre, w_g, (F32,))
    y_s, c_s, n_s, m_s = mlstm_decode(qkv_s[:, :hk], qkv_s[:, hk:2 * hk], qkv_s[:, 2 * hk:], og_s, gt_s,
                                      b_gate, g_head, *state)
    xs = linear_norm_res(y_s, w_out_b, g_post, xs)
    return xp, xs, (c_p, n_p, m_p), (c_s, n_s, m_s)


def _mla_layer(xp, xs, g_pre, g_post, w, pool_lat, pool_kr, layer, page_table, batch, seq):
    nb, n_pages = page_table.shape
    past_len = n_pages * pool_lat.shape[2]
    tm = _tile(batch * seq, ROW_TILE)
    tabs_p = _rope_tables(jnp.arange(seq, dtype=jnp.int32))
    ckv_p, kr_p, q, kf, v = mla_proj(xp, g_pre, w, tabs_p, seq // tm, True, BF16)
    o = mla_attn(q, kf, v, batch, seq)
    xp = linear_norm_res(o, w["o"], g_post, xp)

    tabs_s = _rope_tables(jnp.full((nb,), past_len, jnp.int32))
    ckv_s, kr_s, q_s = mla_proj(xs, g_pre, w, tabs_s, 1, False, BF16)
    q_lat = head_matmul(q_s, w["uk_t"], BF16).reshape(nb, B_HEADS, B_KV_LORA)
    q_rope = q_s.reshape(nb, B_HEADS, HEAD_PAD)[:, :, B_NOPE:B_NOPE + B_ROPE]
    o_lat = mla_decode(q_lat, q_rope, ckv_s, kr_s, pool_lat, pool_kr, layer, page_table)
    o_s = head_matmul(o_lat.reshape(nb, B_HEADS * B_KV_LORA), w["uv_h"], BF16)
    xs = linear_norm_res(o_s, w["o"], g_post, xs)
    return xp, xs, (ckv_p, kr_p), (ckv_s, kr_s)


def _sb_layer(xp, xs, g_pre, g_post, w_qkv, w_o, pool_k, pool_v, layer, page_table, batch, seq):
    hd = C_HEADS * C_HEAD_DIM
    w_q = w_qkv[:, :hd].astype(BF16)
    w_kv = w_qkv[:, hd:].astype(BF16)
    w_o_b = w_o.astype(BF16)
    (q,) = norm_linear(xp, g_pre, w_q, (BF16,))
    kv32, kv16 = norm_linear(xp, g_pre, w_kv, (F32, BF16))
    o = sb_attn(q, kv16[:, :hd], kv16[:, hd:], batch, seq)
    xp = linear_norm_res(o, w_o_b, g_post, xp)

    (qkv_s,) = norm_linear(xs, g_pre, w_qkv.astype(BF16), (F32,))
    q_s, k_s, v_s = qkv_s[:, :hd], qkv_s[:, hd:2 * hd], qkv_s[:, 2 * hd:]
    o_s = sb_decode(q_s, k_s, v_s, pool_k, pool_v, layer, page_table)
    xs = linear_norm_res(o_s, w_o_b, g_post, xs)
    return xp, xs, (kv32[:, :hd], kv32[:, hd:]), (k_s, v_s)


def kernel(x_prompt, x_sample, state_mlstm_C, state_mlstm_n, state_mlstm_m, cache_mla_latent, cache_mla_krope, cache_sb_k, cache_sb_v, page_table, g_mix_pre, g_mix_post, g_ffn_pre, g_ffn_post, w_ffn_in, w_ffn_out, w_a_in, b_a_gate, g_a_head, w_a_out, w_b_down, g_b_q, g_b_kv, w_b_uq, w_b_uk, w_b_uv, w_b_o, w_c_qkv, w_c_o):
    batch, seq, d = x_prompt.shape
    nb = x_sample.shape[0]
    depth = g_mix_pre.shape[0]
    xp = x_prompt.reshape(batch * seq, d)
    xs = x_sample.reshape(nb, d)
    a_p, a_s, b_p, b_s, c_p, c_s = [], [], [], [], [], []
    for i in range(depth):
        kind, j = i % N_MIXERS, i // N_MIXERS
        if kind == 0:
            state = (state_mlstm_C[j], state_mlstm_n[j], state_mlstm_m[j])
            xp, xs, st_p, st_s = _mlstm_layer(xp, xs, g_mix_pre[i], g_mix_post[i], w_a_in[j], b_a_gate[j],
                                              g_a_head[j], w_a_out[j], state, batch, seq)
            a_p.append(st_p)
            a_s.append(st_s)
        elif kind == 1:
            w = _mla_weights(w_b_down[j], g_b_q[j], g_b_kv[j], w_b_uq[j], w_b_uk[j], w_b_uv[j], w_b_o[j])
            xp, xs, st_p, st_s = _mla_layer(xp, xs, g_mix_pre[i], g_mix_post[i], w, cache_mla_latent,
                                            cache_mla_krope, j, page_table, batch, seq)
            b_p.append(st_p)
            b_s.append(st_s)
        else:
            xp, xs, st_p, st_s = _sb_layer(xp, xs, g_mix_pre[i], g_mix_post[i], w_c_qkv[j], w_c_o[j],
                                           cache_sb_k, cache_sb_v, j, page_table, batch, seq)
            c_p.append(st_p)
            c_s.append(st_s)
        w_gate = w_ffn_in[i][:, :D_FF].astype(BF16)
        w_up = w_ffn_in[i][:, D_FF:].astype(BF16)
        w_out = w_ffn_out[i].astype(BF16)
        xp = ffn(xp, g_ffn_pre[i], w_gate, w_up, w_out, g_ffn_post[i])
        xs = ffn(xs, g_ffn_pre[i], w_gate, w_up, w_out, g_ffn_post[i])

    def stack(states, idx, shape):
        return jnp.stack([s[idx].reshape(shape) for s in states])

    return (xp.reshape(batch, seq, d), xs.reshape(nb, 1, d),
            stack(a_p, 0, (batch, A_HEADS, A_DK, A_DV)), stack(a_p, 1, (batch, A_HEADS, A_DK)),
            stack(a_p, 2, (batch, A_HEADS)),
            stack(a_s, 0, (nb, A_HEADS, A_DK, A_DV)), stack(a_s, 1, (nb, A_HEADS, A_DK)),
            stack(a_s, 2, (nb, A_HEADS)),
            stack(b_p, 0, (batch, seq, B_KV_LORA)), stack(b_p, 1, (batch, seq, B_ROPE)),
            stack(b_s, 0, (nb, 1, B_KV_LORA)), stack(b_s, 1, (nb, 1, B_ROPE)),
            stack(c_p, 0, (batch, seq, C_HEADS, C_HEAD_DIM)), stack(c_p, 1, (batch, seq, C_HEADS, C_HEAD_DIM)),
            stack(c_s, 0, (nb, 1, C_HEADS, C_HEAD_DIM)), stack(c_s, 1, (nb, 1, C_HEADS, C_HEAD_DIM)))
```

```python
import functools

import jax
import jax.numpy as jnp
from jax import lax
from jax.experimental import pallas as pl
from jax.experimental.pallas import tpu as pltpu

F32 = jnp.float32
BF16 = jnp.bfloat16

D_MODEL = 1024
N_MIXERS = 3
A_HEADS = 4
A_DK = 128
A_DV = 256
M_INIT = -1e30
B_HEADS = 16
B_NOPE = 64
B_ROPE = 32
B_VDIM = 64
B_Q_LORA = 768
B_KV_LORA = 256
ROPE_THETA = 10000.0
MLA_SCALE = (B_NOPE + B_ROPE) ** -0.5
C_HEADS = 16
C_HEAD_DIM = 64
C_SCALE = C_HEAD_DIM ** -0.5
D_FF = 2816
EPS = 1e-6

LANES = 128
HEAD_PAD = LANES
FFN_CHUNK = 256
NL_CHUNK = 512
VMEM_LIMIT = 56 * 1024 * 1024
ROW_TILE = 512
MLSTM_CHUNK = 256
ATT_TILE = 512
SB_TILE = 256
MLA_DEC_PAGES = 8
ATT_SPLIT = 2
NEG_INF = float("-inf")
SB_STOP = -105.0
LOG2_E = 1.4426950408889634


def _params(*sem):
    return pltpu.CompilerParams(dimension_semantics=sem, vmem_limit_bytes=VMEM_LIMIT)


def _rms(x, g):
    return x * lax.rsqrt(jnp.mean(x * x, axis=-1, keepdims=True) + EPS) * g


def _log_sigmoid(x):
    return -(jnp.maximum(-x, 0.0) + jnp.log1p(jnp.exp(-jnp.abs(x))))


def _dot(a, b):
    return jnp.dot(a, b, preferred_element_type=F32)


def _dot_nt(a, b):
    return lax.dot_general(a, b, (((1,), (1,)), ((), ())), preferred_element_type=F32)


def _tile(n, t):
    t = min(n, t)
    assert n % t == 0, (n, t)
    return t


def _norm_linear_kernel(x_ref, g_ref, *refs, groups):
    h = _rms(x_ref[...], g_ref[...]).astype(BF16)
    n_in = sum(1 + has_scale for _, _, has_scale in groups)
    ins, outs = refs[:n_in], refs[n_in:]
    ii = oi = 0
    for n, n_out, has_scale in groups:
        w_ref = ins[ii]
        s_ref = ins[ii + 1] if has_scale else None
        ii += 1 + has_scale
        for c0 in range(0, n, NL_CHUNK):
            cols = slice(c0, min(n, c0 + NL_CHUNK))
            acc = _dot(h, w_ref[:, cols])
            if has_scale:
                acc = acc * s_ref[:, cols]
            for o in outs[oi:oi + n_out]:
                o[:, cols] = acc.astype(o.dtype)
        oi += n_out


def norm_linear(x, g, groups):
    m, k = x.shape
    tm = _tile(m, ROW_TILE)
    const = lambda i: (0, 0)
    row = lambda i: (i, 0)
    in_specs = [pl.BlockSpec((tm, k), row), pl.BlockSpec((1, k), const)]
    args = [x, g.reshape(1, k)]
    out_specs, out_shape, meta = [], [], []
    for w, out_dtypes, col_scale in groups:
        n = w.shape[1]
        in_specs.append(pl.BlockSpec((k, n), const, pipeline_mode=pl.Buffered(1)))
        args.append(w)
        if col_scale is not None:
            in_specs.append(pl.BlockSpec((1, n), const))
            args.append(col_scale.reshape(1, n))
        for dt in out_dtypes:
            out_specs.append(pl.BlockSpec((tm, n), row))
            out_shape.append(jax.ShapeDtypeStruct((m, n), dt))
        meta.append((n, len(out_dtypes), col_scale is not None))
    outs = pl.pallas_call(
        functools.partial(_norm_linear_kernel, groups=tuple(meta)),
        grid=(m // tm,),
        in_specs=in_specs, out_specs=out_specs, out_shape=out_shape,
        compiler_params=_params("parallel"),
        name="norm_linear",
    )(*args)
    res, oi = [], 0
    for _, n_out, _ in meta:
        res.append(list(outs[oi:oi + n_out]))
        oi += n_out
    return res


def _linear_norm_res_kernel(y_ref, w_ref, g_ref, r_ref, o_ref):
    acc = _dot(y_ref[...].astype(BF16), w_ref[...])
    o_ref[...] = r_ref[...] + _rms(acc, g_ref[...])


def linear_norm_res(y, w, g, res):
    m, k = y.shape
    n = w.shape[1]
    tm = _tile(m, ROW_TILE)
    return pl.pallas_call(
        _linear_norm_res_kernel,
        grid=(m // tm,),
        in_specs=[pl.BlockSpec((tm, k), lambda i: (i, 0)),
                  pl.BlockSpec((k, n), lambda i: (0, 0)),
                  pl.BlockSpec((1, n), lambda i: (0, 0)),
                  pl.BlockSpec((tm, n), lambda i: (i, 0))],
        out_specs=pl.BlockSpec((tm, n), lambda i: (i, 0)),
        out_shape=jax.ShapeDtypeStruct((m, n), F32),
        compiler_params=_params("parallel"),
        name="linear_norm_res",
    )(y, w, g.reshape(1, n), res)


def _ffn_kernel(x_ref, gp_ref, wg_ref, wu_ref, wo_ref, gq_ref, o_ref):
    x = x_ref[...]
    h = _rms(x, gp_ref[...]).astype(BF16)
    acc = jnp.zeros(x.shape, F32)
    for c in range(D_FF // FFN_CHUNK):
        sl = slice(c * FFN_CHUNK, (c + 1) * FFN_CHUNK)
        a = _dot(h, wg_ref[:, sl])
        u = _dot(h, wu_ref[:, sl])
        act = (a * jax.nn.sigmoid(a) * u).astype(BF16)
        acc = acc + _dot(act, wo_ref[sl, :])
    o_ref[...] = x + _rms(acc, gq_ref[...])


def ffn(x, g_pre, w_gate, w_up, w_out, g_post):
    m, d = x.shape
    tm = _tile(m, ROW_TILE)
    const = lambda i: (0, 0)
    return pl.pallas_call(
        _ffn_kernel,
        grid=(m // tm,),
        in_specs=[pl.BlockSpec((tm, d), lambda i: (i, 0)),
                  pl.BlockSpec((1, d), const),
                  pl.BlockSpec((d, D_FF), const, pipeline_mode=pl.Buffered(1)),
                  pl.BlockSpec((d, D_FF), const, pipeline_mode=pl.Buffered(1)),
                  pl.BlockSpec((D_FF, d), const, pipeline_mode=pl.Buffered(1)),
                  pl.BlockSpec((1, d), const)],
        out_specs=pl.BlockSpec((tm, d), lambda i: (i, 0)),
        out_shape=jax.ShapeDtypeStruct((m, d), F32),
        compiler_params=_params("parallel"),
        name="ffn",
    )(x, g_pre.reshape(1, d), w_gate, w_up, w_out, g_post.reshape(1, d))


def _mlstm_prompt_kernel(q_ref, k_ref, v_ref, o_ref, gt_ref, bg_ref, gh_ref,
                         y_ref, c_ref, n_ref, m_ref, *, chunk):
    L = chunk

    @pl.when(pl.program_id(1) == 0)
    def _():
        c_ref[...] = jnp.zeros(c_ref.shape, F32)
        n_ref[...] = jnp.zeros(n_ref.shape, F32)
        m_ref[...] = jnp.full(m_ref.shape, M_INIT, F32)

    gates = gt_ref[...] + bg_ref[...]
    lane = lax.broadcasted_iota(jnp.int32, (L, LANES), 1)
    log_f = jnp.where(lane >= A_HEADS, _log_sigmoid(gates), 0.0)
    row = lax.broadcasted_iota(jnp.int32, (L, L), 0)
    col = lax.broadcasted_iota(jnp.int32, (L, L), 1)
    causal = col <= row
    tri = jnp.where(causal, 1.0, 0.0).astype(F32)
    b_all = jnp.dot(tri, log_f, precision=lax.Precision.HIGHEST, preferred_element_type=F32)
    packed = jnp.where(lane < A_HEADS, gates, b_all)
    packed_t = packed.T

    for h in range(A_HEADS):
        li_col = packed[:, h:h + 1]
        b_col = packed[:, A_HEADS + h:A_HEADS + h + 1]
        li_row = packed_t[h:h + 1, :]
        b_row = packed_t[A_HEADS + h:A_HEADS + h + 1, :]
        q = q_ref[:, h * A_DK:(h + 1) * A_DK]
        k = k_ref[:, h * A_DK:(h + 1) * A_DK]
        v = v_ref[:, h * A_DV:(h + 1) * A_DV]
        m_prev = m_ref[0, :, h:h + 1]
        c_mem = c_ref[0, h]
        n_row = n_ref[0, h:h + 1, :]

        log_d = jnp.where(causal, b_col - b_row + li_row, NEG_INF)
        log_inter = b_col + m_prev
        m_t = jnp.maximum(log_inter, jnp.max(log_d, axis=1, keepdims=True))
        s = _dot_nt(q, k) * jnp.exp(log_d - m_t)
        w_inter = jnp.exp(log_inter - m_t)
        num = w_inter * _dot(q, c_mem.astype(BF16)) + _dot(s.astype(BF16), v)
        qn = jnp.sum(q.astype(F32) * n_row, axis=1, keepdims=True)
        den = w_inter * qn + jnp.sum(s, axis=1, keepdims=True)
        hh = num / jnp.maximum(jnp.abs(den), jnp.exp(-m_t))
        hn = _rms(hh, gh_ref[:, h * A_DV:(h + 1) * A_DV])
        gate = jax.nn.sigmoid(o_ref[:, h * A_DV:(h + 1) * A_DV])
        y_ref[:, h * A_DV:(h + 1) * A_DV] = (gate * hn).astype(y_ref.dtype)

        m_new = m_t[L - 1:L, :]
        b_last = b_col[L - 1:L, :]
        w_col = jnp.exp(b_last - b_col + li_col - m_new)
        decay = jnp.exp(b_last + m_prev - m_new)
        kw = k.astype(F32) * w_col
        c_ref[0, h] = decay * c_mem + _dot(kw.T.astype(BF16), v)
        n_ref[0, h:h + 1, :] = decay * n_row + jnp.sum(kw, axis=0, keepdims=True)
        m_ref[0, :, h:h + 1] = m_new


def mlstm_prompt(qkv, o_gate, gates, b_gate, g_head, batch, seq):
    L = _tile(seq, MLSTM_CHUNK)
    nc = seq // L
    hk = A_HEADS * A_DK
    hv = A_HEADS * A_DV
    m = batch * seq
    row_map = lambda b, c: (b * nc + c, 0)
    const = lambda b, c: (0, 0)
    bg = jnp.zeros((1, LANES), F32).at[0, :2 * A_HEADS].set(b_gate)
    y, c_fin, n_fin, m_fin = pl.pallas_call(
        functools.partial(_mlstm_prompt_kernel, chunk=L),
        grid=(batch, nc),
        in_specs=[pl.BlockSpec((L, hk), row_map),
                  pl.BlockSpec((L, hk), lambda b, c: (b * nc + c, 1)),
                  pl.BlockSpec((L, hv), lambda b, c: (b * nc + c, 1)),
                  pl.BlockSpec((L, hv), row_map),
                  pl.BlockSpec((L, LANES), row_map),
                  pl.BlockSpec((1, LANES), const),
                  pl.BlockSpec((1, hv), const)],
        out_specs=[pl.BlockSpec((L, hv), row_map),
                   pl.BlockSpec((1, A_HEADS, A_DK, A_DV), lambda b, c: (b, 0, 0, 0)),
                   pl.BlockSpec((1, A_HEADS, A_DK), lambda b, c: (b, 0, 0)),
                   pl.BlockSpec((1, 1, LANES), lambda b, c: (b, 0, 0))],
        out_shape=[jax.ShapeDtypeStruct((m, hv), BF16),
                   jax.ShapeDtypeStruct((batch, A_HEADS, A_DK, A_DV), F32),
                   jax.ShapeDtypeStruct((batch, A_HEADS, A_DK), F32),
                   jax.ShapeDtypeStruct((batch, 1, LANES), F32)],
        compiler_params=_params("parallel", "arbitrary"),
        name="mlstm_prompt",
    )(qkv, qkv, qkv, o_gate, gates, bg, g_head.reshape(1, hv))
    return y, c_fin, n_fin, m_fin[:, 0, :A_HEADS]


def _mlstm_decode_kernel(q_ref, k_ref, v_ref, o_ref, gt_ref, bg_ref, gh_ref, c_ref, n_ref, m_ref,
                         y_ref, co_ref, no_ref, mo_ref, *, tb):
    r = lax.broadcasted_iota(jnp.int32, (A_DK, A_DK), 0)
    c = lax.broadcasted_iota(jnp.int32, (A_DK, A_DK), 1)
    eye = r == c
    lane = lax.broadcasted_iota(jnp.int32, (1, LANES), 1)

    def to_col(x_row):
        return jnp.sum(jnp.where(eye, jnp.broadcast_to(x_row, (A_DK, A_DK)), 0.0), axis=1, keepdims=True)

    def body(bb, carry):
        gates = gt_ref[bb] + bg_ref[...]
        log_f = _log_sigmoid(gates)
        m_row = m_ref[bb]
        m_out = jnp.zeros((1, LANES), F32)
        q_all = q_ref[bb]
        k_all = k_ref[bb]
        v_all = v_ref[bb]
        o_all = o_ref[bb]
        for h in range(A_HEADS):
            li = gates[:, h:h + 1]
            lf = log_f[:, A_HEADS + h:A_HEADS + h + 1]
            m_prev = m_row[:, h:h + 1]
            m_t = jnp.maximum(lf + m_prev, li)
            w_inter = jnp.exp(lf + m_prev - m_t)
            w_new = jnp.exp(li - m_t)
            q = q_all[:, h * A_DK:(h + 1) * A_DK]
            k = k_all[:, h * A_DK:(h + 1) * A_DK]
            v = v_all[:, h * A_DV:(h + 1) * A_DV]
            c_mem = c_ref[bb, h]
            n_row = n_ref[bb, pl.ds(h, 1), :]
            s = jnp.sum(q * k, axis=1, keepdims=True) * w_new
            qc = jnp.sum(c_mem * to_col(q), axis=0, keepdims=True)
            num = w_inter * qc + s * v
            den = w_inter * jnp.sum(q * n_row, axis=1, keepdims=True) + s
            hh = num / jnp.maximum(jnp.abs(den), jnp.exp(-m_t))
            hn = _rms(hh, gh_ref[:, h * A_DV:(h + 1) * A_DV])
            gate = jax.nn.sigmoid(o_all[:, h * A_DV:(h + 1) * A_DV])
            y_ref[bb, :, h * A_DV:(h + 1) * A_DV] = gate * hn
            kw = k * w_new
            co_ref[bb, h] = w_inter * c_mem + to_col(kw) * v
            no_ref[bb, pl.ds(h, 1), :] = w_inter * n_row + kw
            m_out = jnp.where(lane == h, m_t, m_out)
        mo_ref[bb] = m_out
        return carry

    lax.fori_loop(0, tb, body, 0)


def mlstm_decode(q, k, v, o_gate, gates, b_gate, g_head, c0, n0, m0):
    nb = q.shape[0]
    tb = _tile(nb, 8)
    hk = A_HEADS * A_DK
    hv = A_HEADS * A_DV
    bg = jnp.zeros((1, LANES), F32).at[0, :2 * A_HEADS].set(b_gate)
    m_pad = jnp.zeros((nb, 1, LANES), F32).at[:, 0, :A_HEADS].set(m0)
    row3 = lambda i: (i, 0, 0)
    const = lambda i: (0, 0)
    y, c1, n1, m1 = pl.pallas_call(
        functools.partial(_mlstm_decode_kernel, tb=tb),
        grid=(nb // tb,),
        in_specs=[pl.BlockSpec((tb, 1, hk), row3),
                  pl.BlockSpec((tb, 1, hk), row3),
                  pl.BlockSpec((tb, 1, hv), row3),
                  pl.BlockSpec((tb, 1, hv), row3),
                  pl.BlockSpec((tb, 1, LANES), row3),
                  pl.BlockSpec((1, LANES), const),
                  pl.BlockSpec((1, hv), const),
                  pl.BlockSpec((tb, A_HEADS, A_DK, A_DV), lambda i: (i, 0, 0, 0)),
                  pl.BlockSpec((tb, A_HEADS, A_DK), row3),
                  pl.BlockSpec((tb, 1, LANES), row3)],
        out_specs=[pl.BlockSpec((tb, 1, hv), row3),
                   pl.BlockSpec((tb, A_HEADS, A_DK, A_DV), lambda i: (i, 0, 0, 0)),
                   pl.BlockSpec((tb, A_HEADS, A_DK), row3),
                   pl.BlockSpec((tb, 1, LANES), row3)],
        out_shape=[jax.ShapeDtypeStruct((nb, 1, hv), F32),
                   jax.ShapeDtypeStruct((nb, A_HEADS, A_DK, A_DV), F32),
                   jax.ShapeDtypeStruct((nb, A_HEADS, A_DK), F32),
                   jax.ShapeDtypeStruct((nb, 1, LANES), F32)],
        compiler_params=_params("parallel"),
        name="mlstm_decode",
    )(q.reshape(nb, 1, hk), k.reshape(nb, 1, hk), v.reshape(nb, 1, hv), o_gate.reshape(nb, 1, hv),
      gates.reshape(nb, 1, LANES), bg, g_head.reshape(1, hv), c0, n0, m_pad)
    return y.reshape(nb, hv), c1, n1, m1[:, 0, :A_HEADS]


def _rope_tables(pos):
    half = B_ROPE // 2
    inv_freq = ROPE_THETA ** (-jnp.arange(half, dtype=F32) / half)
    ang = pos.astype(F32)[:, None] * inv_freq[None, :]
    cos, sin = jnp.cos(ang), jnp.sin(ang)
    cos_k = jnp.concatenate([cos, cos], axis=1)
    sin_k = jnp.concatenate([-sin, sin], axis=1)
    ones = jnp.ones((pos.shape[0], B_NOPE), F32)
    tail = HEAD_PAD - B_NOPE - B_ROPE
    cos_q = jnp.concatenate([ones, cos_k, jnp.ones((pos.shape[0], tail), F32)], axis=1)
    sin_q = jnp.concatenate([0 * ones, sin_k, jnp.zeros((pos.shape[0], tail), F32)], axis=1)
    return cos_k, sin_k, cos_q, sin_q


def _mla_proj_kernel(*refs, want_kv):
    (x_ref, gp_ref, wd_ref, gq_ref, gkv_ref, ck_ref, sk_ref, cq_ref, sq_ref, wuq_ref) = refs[:10]
    if want_kv:
        wk_ref, we_ref, wv_ref = refs[10:13]
        ckv_ref, kr_ref, q_ref, kf_ref, v_ref = refs[13:]
    else:
        ckv_ref, kr_ref, q_ref = refs[10:]
    half = B_ROPE // 2
    h = _rms(x_ref[...], gp_ref[...]).astype(BF16)
    d = _dot(h, wd_ref[...])
    c_q = _rms(d[:, :B_Q_LORA], gq_ref[...]).astype(BF16)
    c_kv = _rms(d[:, B_Q_LORA:B_Q_LORA + B_KV_LORA], gkv_ref[...])
    ckv_ref[...] = c_kv
    kr = d[:, B_Q_LORA + B_KV_LORA:B_Q_LORA + B_KV_LORA + B_ROPE]
    kr_rot = jnp.concatenate([kr[:, half:], kr[:, :half]], axis=1)
    k_rope = kr * ck_ref[...] + kr_rot * sk_ref[...]
    kr_ref[...] = k_rope

    q = _dot(c_q, wuq_ref[...])
    lane = lax.broadcasted_iota(jnp.int32, (q.shape[0], HEAD_PAD), 1)
    first = lane < B_NOPE + half
    cq, sq = cq_ref[...], sq_ref[...]
    for hd in range(B_HEADS):
        xg = q[:, hd * HEAD_PAD:(hd + 1) * HEAD_PAD]
        rot = jnp.where(first, pltpu.roll(xg, HEAD_PAD - half, 1), pltpu.roll(xg, half, 1))
        q_ref[:, hd * HEAD_PAD:(hd + 1) * HEAD_PAD] = (xg * cq + rot * sq).astype(q_ref.dtype)

    if want_kv:
        c_kv_b = c_kv.astype(BF16)
        kr_pad = jnp.concatenate([k_rope, jnp.zeros((k_rope.shape[0], LANES - B_ROPE), F32)], axis=1).astype(BF16)
        kf_ref[...] = (_dot(c_kv_b, wk_ref[...]) + _dot(kr_pad, we_ref[...])).astype(BF16)
        v_ref[0] = _dot_nt(wv_ref[...], c_kv_b).astype(BF16)


def mla_proj(x, g_pre, w, tabs, n_pos_blocks, want_kv, q_dtype, tm):
    m, d = x.shape
    hp = B_HEADS * HEAD_PAD
    cos_k, sin_k, cos_q, sin_q = tabs
    assert cos_k.shape[0] == n_pos_blocks * tm
    const = lambda i: (0, 0)
    row = lambda i: (i, 0)
    posm = lambda i: (i % n_pos_blocks, 0)
    in_specs = [pl.BlockSpec((tm, d), row),
                pl.BlockSpec((1, d), const),
                pl.BlockSpec(w["down"].shape, const),
                pl.BlockSpec((1, B_Q_LORA), const),
                pl.BlockSpec((1, B_KV_LORA), const),
                pl.BlockSpec((tm, B_ROPE), posm),
                pl.BlockSpec((tm, B_ROPE), posm),
                pl.BlockSpec((tm, HEAD_PAD), posm),
                pl.BlockSpec((tm, HEAD_PAD), posm),
                pl.BlockSpec(w["uq"].shape, const)]
    args = [x, g_pre.reshape(1, d), w["down"], w["g_q"], w["g_kv"], cos_k, sin_k, cos_q, sin_q, w["uq"]]
    out_specs = [pl.BlockSpec((tm, B_KV_LORA), row),
                 pl.BlockSpec((tm, B_ROPE), row),
                 pl.BlockSpec((tm, hp), row)]
    out_shape = [jax.ShapeDtypeStruct((m, B_KV_LORA), F32),
                 jax.ShapeDtypeStruct((m, B_ROPE), F32),
                 jax.ShapeDtypeStruct((m, hp), q_dtype)]
    if want_kv:
        in_specs += [pl.BlockSpec(w["uk"].shape, const),
                     pl.BlockSpec(w["rope_place"].shape, const),
                     pl.BlockSpec(w["uv_t"].shape, const)]
        args += [w["uk"], w["rope_place"], w["uv_t"]]
        out_specs += [pl.BlockSpec((tm, hp), row), pl.BlockSpec((1, hp, tm), lambda i: (i, 0, 0))]
        out_shape += [jax.ShapeDtypeStruct((m, hp), BF16), jax.ShapeDtypeStruct((m // tm, hp, tm), BF16)]
    return pl.pallas_call(
        functools.partial(_mla_proj_kernel, want_kv=want_kv),
        grid=(m // tm,),
        in_specs=in_specs, out_specs=out_specs, out_shape=out_shape,
        compiler_params=_params("parallel"),
        name="mla_proj",
    )(*args)


def _tree_reduce(x, op):
    rows, w = x.shape
    x = x.reshape(rows // 8, 8, w)
    while x.shape[0] > 1:
        n = x.shape[0] // 2
        x = op(x[:n], x[n:])
    red = jnp.max if op is jnp.maximum else jnp.sum
    return red(x[0], axis=0, keepdims=True)


def _mla_attn_kernel(q_ref, k_ref, vt_ref, o_ref, s0_scr, s1_scr, m_scr, l_scr, acc_scr, *, t):
    i = pl.program_id(2)
    q_t = q_ref[...].astype(F32).T.astype(BF16)
    m_scr[...] = jnp.full(m_scr.shape, NEG_INF, F32)
    l_scr[...] = jnp.zeros(l_scr.shape, F32)
    acc_scr[...] = jnp.zeros(acc_scr.shape, F32)
    w = t // ATT_SPLIT
    key = lax.broadcasted_iota(jnp.int32, (t, w), 0)
    qry = lax.broadcasted_iota(jnp.int32, (t, w), 1)
    log2_scale = MLA_SCALE * LOG2_E

    def scores(j, s_scr):
        start = pl.multiple_of(j * t, t)
        s_scr[...] = _dot(k_ref[pl.ds(start, t), :], q_t)

    def update(j, s_scr, masked):
        vtb = vt_ref[j]
        for c in range(ATT_SPLIT):
            cols = slice(c * w, (c + 1) * w)
            s_t = s_scr[:, cols]
            if masked:
                s_t = jnp.where(key <= qry + c * w, s_t, NEG_INF)
            m_old = m_scr[:, cols]
            m_new = jnp.maximum(m_old, _tree_reduce(s_t, jnp.maximum))
            alpha = jnp.exp2((m_old - m_new) * log2_scale)
            p = jnp.exp2((s_t - m_new) * log2_scale)
            l_scr[:, cols] = alpha * l_scr[:, cols] + _tree_reduce(p, jnp.add)
            acc_scr[:, cols] = alpha * acc_scr[:, cols] + _dot(vtb, p.astype(BF16))
            m_scr[:, cols] = m_new

    scores(0, s0_scr)

    def body(r, carry):
        j = 2 * r
        scores(j + 1, s1_scr)
        update(j, s0_scr, False)
        scores(j + 2, s0_scr)
        update(j + 1, s1_scr, False)
        return carry

    lax.fori_loop(0, i // 2, body, 0)

    @pl.when(i % 2 == 0)
    def _():
        update(i, s0_scr, True)

    @pl.when(i % 2 == 1)
    def _():
        scores(i, s1_scr)
        update(i - 1, s0_scr, False)
        update(i, s1_scr, True)

    o_ref[...] = (acc_scr[...] / l_scr[...]).T.astype(o_ref.dtype)


def mla_attn(q, kf, vt, batch, seq, t):
    nq = seq // t
    m = batch * seq
    hp = B_HEADS * HEAD_PAD
    return pl.pallas_call(
        functools.partial(_mla_attn_kernel, t=t),
        grid=(batch, B_HEADS, nq),
        in_specs=[pl.BlockSpec((t, HEAD_PAD), lambda b, h, i: (b * nq + i, h)),
                  pl.BlockSpec((seq, HEAD_PAD), lambda b, h, i: (b, h)),
                  pl.BlockSpec((nq, HEAD_PAD, t), lambda b, h, i: (b, h, 0))],
        out_specs=pl.BlockSpec((t, HEAD_PAD), lambda b, h, i: (b * nq + i, h)),
        out_shape=jax.ShapeDtypeStruct((m, hp), BF16),
        scratch_shapes=[pltpu.VMEM((t, t), F32), pltpu.VMEM((t, t), F32), pltpu.VMEM((1, t), F32), pltpu.VMEM((1, t), F32),
                        pltpu.VMEM((HEAD_PAD, t), F32)],
        compiler_params=_params("parallel", "parallel", "arbitrary"),
        name="mla_attn",
    )(q, kf, vt)


def _head_matmul_kernel(x_ref, w_ref, o_ref):
    o_ref[...] = _dot(x_ref[...].astype(BF16), w_ref[0]).astype(o_ref.dtype)


def head_matmul(x, w, out_dtype):
    nb = x.shape[0]
    nh, k, n = w.shape
    return pl.pallas_call(
        _head_matmul_kernel,
        grid=(nh,),
        in_specs=[pl.BlockSpec((nb, k), lambda h: (0, h)),
                  pl.BlockSpec((1, k, n), lambda h: (h, 0, 0))],
        out_specs=pl.BlockSpec((nb, n), lambda h: (0, h)),
        out_shape=jax.ShapeDtypeStruct((nb, nh * n), out_dtype),
        compiler_params=_params("parallel"),
        name="head_matmul",
    )(x, w)


def _mla_decode_kernel(pt_ref, ql_ref, qr_ref, cn_ref, kn_ref, lat_hbm, kr_hbm, o_ref,
                       lat_buf, kr_buf, s_scr, sem, *, layer, n_pages, n_seq, group):
    n_groups = n_pages // group
    page = lat_buf.shape[2]

    def page_copies(b, slot, g):
        pid = pt_ref[b * n_pages + g]
        return (pltpu.make_async_copy(lat_hbm.at[layer, pid], lat_buf.at[slot, g], sem.at[0, slot]),
                pltpu.make_async_copy(kr_hbm.at[layer, pid], kr_buf.at[slot, g], sem.at[1, slot]))

    def start_seq(b, slot):
        def one(g, carry):
            for cp in page_copies(b, slot, g):
                cp.start()
            return carry

        lax.fori_loop(0, n_pages, one, 0)

    def wait_seq(b, slot):
        def one(g, carry):
            for cp in page_copies(b, slot, g):
                cp.wait()
            return carry

        lax.fori_loop(0, n_pages, one, 0)

    start_seq(0, 0)

    def seq_body(b, carry):
        slot = b % 2

        @pl.when(b + 1 < n_seq)
        def _():
            start_seq(b + 1, 1 - slot)

        wait_seq(b, slot)
        ql = ql_ref[b]
        qr = qr_ref[b]
        cn = cn_ref[b]
        kn = kn_ref[b]
        s_new = (jnp.sum(ql.astype(F32) * cn, axis=1, keepdims=True)
                 + jnp.sum(qr.astype(F32) * kn, axis=1, keepdims=True)) * MLA_SCALE

        def pass_scores(grp, m):
            parts = []
            for gi in range(group):
                g = grp * group + gi
                lat = lat_buf[slot, g].astype(BF16)
                kr = kr_buf[slot, g].astype(BF16)
                parts.append(_dot_nt(ql, lat) + _dot(qr, kr))
            s = jnp.concatenate(parts, axis=1) * MLA_SCALE
            s_scr[grp] = s
            return jnp.maximum(m, jnp.max(s, axis=1, keepdims=True))

        m = lax.fori_loop(0, n_groups, pass_scores, s_new)

        def pass_values(grp, state):
            l, acc = state
            p = jnp.exp(s_scr[grp] - m)
            pb = p.astype(BF16)
            for gi in range(group):
                lat = lat_buf[slot, grp * group + gi].astype(BF16)
                acc = acc + _dot(pb[:, gi * page:(gi + 1) * page], lat)
            return l + jnp.sum(p, axis=1, keepdims=True), acc

        p_new = jnp.exp(s_new - m)
        l, acc = lax.fori_loop(0, n_groups, pass_values, (p_new, p_new * cn))
        o_ref[b] = acc / l
        return carry

    lax.fori_loop(0, n_seq, seq_body, 0)


def mla_decode(q_lat, q_rope, c_new, k_new, pool_lat, pool_kr, layer, page_table):
    nb, n_pages = page_table.shape
    page = pool_lat.shape[2]
    group = _tile(n_pages, MLA_DEC_PAGES)
    full = lambda i, pt: (0, 0, 0)
    grid_spec = pltpu.PrefetchScalarGridSpec(
        num_scalar_prefetch=1,
        grid=(1,),
        in_specs=[pl.BlockSpec((nb, B_HEADS, B_KV_LORA), full),
                  pl.BlockSpec((nb, B_HEADS, B_ROPE), full),
                  pl.BlockSpec((nb, 1, B_KV_LORA), full),
                  pl.BlockSpec((nb, 1, B_ROPE), full),
                  pl.BlockSpec(memory_space=pl.ANY),
                  pl.BlockSpec(memory_space=pl.ANY)],
        out_specs=pl.BlockSpec((nb, B_HEADS, B_KV_LORA), full),
        scratch_shapes=[pltpu.VMEM((2, n_pages, page, B_KV_LORA), F32),
                        pltpu.VMEM((2, n_pages, B_ROPE, page), F32),
                        pltpu.VMEM((n_pages // group, B_HEADS, group * page), F32),
                        pltpu.SemaphoreType.DMA((2, 2))])
    return pl.pallas_call(
        functools.partial(_mla_decode_kernel, layer=layer, n_pages=n_pages, n_seq=nb, group=group),
        grid_spec=grid_spec,
        out_shape=jax.ShapeDtypeStruct((nb, B_HEADS, B_KV_LORA), F32),
        compiler_params=_params("arbitrary"),
        name="mla_decode",
    )(page_table.reshape(-1), q_lat, q_rope, c_new.reshape(nb, 1, B_KV_LORA), k_new.reshape(nb, 1, B_ROPE),
      pool_lat, jnp.transpose(pool_kr, (0, 1, 3, 2)))


def _suffix_sums(log_rest, ones_tri):
    hi = log_rest.astype(BF16)
    lo = (log_rest - hi.astype(F32)).astype(BF16)
    return _dot(hi, ones_tri) + _dot(lo, ones_tri)


def _sb_attn_kernel(q_ref, k_ref, v_ref, o_ref, acc_scr, c_scr, *, t):
    i = pl.program_id(2)
    half = C_HEAD_DIM
    lane = lax.broadcasted_iota(jnp.int32, (t, 2 * half), 1)
    qs = q_ref[...] * jnp.asarray(C_SCALE, BF16)
    zero = jnp.zeros_like(qs)
    q_heads = (jnp.where(lane < half, qs, zero), jnp.where(lane >= half, qs, zero))
    row = lax.broadcasted_iota(jnp.int32, (t, t), 0)
    col = lax.broadcasted_iota(jnp.int32, (t, t), 1)
    ones_tri = jnp.where(row >= col, 1.0, 0.0).astype(BF16)
    valid = col < row
    acc_scr[...] = jnp.zeros(acc_scr.shape, F32)

    def tile(j, masked, c_heads):
        start = pl.multiple_of(j * t, t)
        kb = k_ref[pl.ds(start, t), :]
        vb = v_ref[pl.ds(start, t), :]
        c_out = []
        for hh in range(2):
            z = _dot_nt(q_heads[hh], kb)
            log_rest = _log_sigmoid(-z)
            if masked:
                log_rest = jnp.where(valid, log_rest, 0.0)
            tail = _suffix_sums(log_rest, ones_tri)
            a = jnp.exp(z + tail + c_heads[hh])
            if masked:
                a = jnp.where(valid, a, 0.0)
            acc_scr[hh] = acc_scr[hh] + _dot(a.astype(BF16), vb)
            c_out.append(c_heads[hh] + tail[:, 0:1])
        return c_out

    none = jnp.zeros((t, 1), F32)
    c_diag = tile(i, True, [none, none])
    gone = jnp.where(i > 0, 0.0, NEG_INF)
    c_prev = tile(jnp.maximum(i - 1, 0), False, [c + gone for c in c_diag])
    c_scr[0] = c_prev[0]
    c_scr[1] = c_prev[1]

    def live():
        return (jnp.max(jnp.maximum(c_scr[0], c_scr[1])) > SB_STOP).astype(jnp.int32)

    def cond(state):
        r, go = state
        return jnp.logical_and(r < i - 1, go > 0)

    def body(state):
        r, _ = state
        c_new = tile(i - 2 - r, False, [c_scr[0], c_scr[1]])
        c_scr[0] = c_new[0]
        c_scr[1] = c_new[1]
        return r + 1, live()

    lax.while_loop(cond, body, (jnp.int32(0), live()))
    o_ref[...] = jnp.where(lane < half, acc_scr[0], acc_scr[1]).astype(o_ref.dtype)


def sb_attn(q, k, v, batch, seq):
    t = _tile(seq, SB_TILE)
    nq = seq // t
    m = batch * seq
    w = 2 * C_HEAD_DIM
    return pl.pallas_call(
        functools.partial(_sb_attn_kernel, t=t),
        grid=(batch, C_HEADS // 2, nq),
        in_specs=[pl.BlockSpec((t, w), lambda b, h, i: (b * nq + i, h)),
                  pl.BlockSpec((seq, w), lambda b, h, i: (b, h)),
                  pl.BlockSpec((seq, w), lambda b, h, i: (b, h))],
        out_specs=pl.BlockSpec((t, w), lambda b, h, i: (b * nq + i, h)),
        out_shape=jax.ShapeDtypeStruct((m, C_HEADS * C_HEAD_DIM), BF16),
        scratch_shapes=[pltpu.VMEM((2, t, w), F32), pltpu.VMEM((2, t, 1), F32)],
        compiler_params=_params("parallel", "parallel", "arbitrary"),
        name="sb_attn",
    )(q, k, v)


def _sb_decode_kernel(pt_ref, q_ref, kn_ref, vn_ref, pk_ref, pv_ref, o_ref,
                      kbuf, vbuf, sem, c_scr, acc_scr, *, layer, n_pages, n_seq, new_valid):
    hd = C_HEADS * C_HEAD_DIM
    page = kbuf.shape[2]
    head_of_lane = lax.broadcasted_iota(jnp.int32, (C_HEADS, hd), 1) // C_HEAD_DIM
    own = head_of_lane == lax.broadcasted_iota(jnp.int32, (C_HEADS, hd), 0)
    row = lax.broadcasted_iota(jnp.int32, (page, page), 0)
    col = lax.broadcasted_iota(jnp.int32, (page, page), 1)
    ones_tri = jnp.where(row >= col, 1.0, 0.0).astype(BF16)

    def page_copies(b, j, slot):
        pid = pt_ref[b * n_pages + n_pages - 1 - j]
        return (pltpu.make_async_copy(pk_ref.at[layer, pid], kbuf.at[slot], sem.at[0, slot]),
                pltpu.make_async_copy(pv_ref.at[layer, pid], vbuf.at[slot], sem.at[1, slot]))

    def start(b, j, slot):
        for cp in page_copies(b, j, slot):
            cp.start()

    def wait(b, j, slot):
        for cp in page_copies(b, j, slot):
            cp.wait()

    def live():
        return (jnp.max(c_scr[...]) > SB_STOP).astype(jnp.int32)

    start(0, 0, 0)

    def seq_body(b, carry):
        first = b % 2

        @pl.when(b + 1 < n_seq)
        def _():
            start(b + 1, 0, 1 - first)

        q_row = q_ref[b] * C_SCALE
        q_bd = jnp.where(own, jnp.broadcast_to(q_row, (C_HEADS, hd)), 0.0).astype(BF16)
        z_new = jnp.sum(jnp.where(own, q_row * kn_ref[b], 0.0), axis=1, keepdims=True)
        a_new = jnp.where(new_valid, jnp.exp(_log_sigmoid(z_new)), 0.0)
        c_scr[...] = jnp.where(new_valid, _log_sigmoid(-z_new), 0.0)
        acc_scr[...] = a_new * jnp.broadcast_to(vn_ref[b], acc_scr.shape)

        def process(slot):
            kt = kbuf[slot].astype(BF16)
            vt = vbuf[slot].astype(BF16)
            z = _dot(q_bd, kt)
            tail = _suffix_sums(_log_sigmoid(-z), ones_tri)
            a = jnp.exp(z + tail + c_scr[...])
            acc_scr[...] = acc_scr[...] + _dot_nt(a.astype(BF16), vt)
            c_scr[...] = c_scr[...] + tail[:, 0:1]

        wait(b, 0, first)
        if n_pages > 1:
            start(b, 1, 3)
        process(first)

        def cond(state):
            j, go = state
            return jnp.logical_and(j < n_pages, go > 0)

        def body(state):
            j, _ = state
            slot = 2 + j % 2
            wait(b, j, slot)

            @pl.when(j + 1 < n_pages)
            def _():
                start(b, j + 1, 2 + (j + 1) % 2)

            process(slot)
            return j + 1, live()

        j_end, _ = lax.while_loop(cond, body, (jnp.int32(1), live()))

        @pl.when(j_end < n_pages)
        def _():
            wait(b, j_end, 2 + j_end % 2)

        o_ref[b] = jnp.sum(jnp.where(own, acc_scr[...], 0.0), axis=0, keepdims=True)
        return carry

    lax.fori_loop(0, n_seq, seq_body, 0)


def sb_decode(q, k_new, v_new, pool_k, pool_v, layer, page_table):
    nb, n_pages = page_table.shape
    page = pool_k.shape[2]
    hd = C_HEADS * C_HEAD_DIM
    pk = jnp.transpose(pool_k, (0, 1, 3, 4, 2)).reshape(pool_k.shape[0], pool_k.shape[1], hd, page)
    pv = jnp.transpose(pool_v, (0, 1, 3, 4, 2)).reshape(pool_v.shape[0], pool_v.shape[1], hd, page)
    past_len = n_pages * page
    new_valid = past_len < past_len
    full = lambda i, pt: (0, 0, 0)
    grid_spec = pltpu.PrefetchScalarGridSpec(
        num_scalar_prefetch=1,
        grid=(1,),
        in_specs=[pl.BlockSpec((nb, 1, hd), full)] * 3 + [pl.BlockSpec(memory_space=pl.ANY)] * 2,
        out_specs=pl.BlockSpec((nb, 1, hd), full),
        scratch_shapes=[pltpu.VMEM((4, hd, page), F32), pltpu.VMEM((4, hd, page), F32),
                        pltpu.SemaphoreType.DMA((2, 4)),
                        pltpu.VMEM((C_HEADS, 1), F32), pltpu.VMEM((C_HEADS, hd), F32)])
    o = pl.pallas_call(
        functools.partial(_sb_decode_kernel, layer=layer, n_pages=n_pages, n_seq=nb, new_valid=new_valid),
        grid_spec=grid_spec,
        out_shape=jax.ShapeDtypeStruct((nb, 1, hd), F32),
        compiler_params=_params("arbitrary"),
        name="sb_decode",
    )(page_table.reshape(-1), q.reshape(nb, 1, hd), k_new.reshape(nb, 1, hd), v_new.reshape(nb, 1, hd), pk, pv)
    return o.reshape(nb, hd)


def _pad_heads(w, n_heads, width):
    k = w.shape[0]
    w = w.reshape(k, n_heads, width)
    return jnp.pad(w, ((0, 0), (0, 0), (0, HEAD_PAD - width))).reshape(k, n_heads * HEAD_PAD)


def _mla_weights(w_down, g_q, g_kv, w_uq, w_uk, w_uv, w_o):
    d_down = B_Q_LORA + B_KV_LORA + B_ROPE
    down_pad = -d_down % LANES
    place = jnp.zeros((LANES, B_HEADS, HEAD_PAD), F32)
    idx = jnp.arange(B_ROPE)
    place = place.at[idx, :, B_NOPE + idx].set(1.0).reshape(LANES, B_HEADS * HEAD_PAD)
    uk2 = w_uk.reshape(B_KV_LORA, B_HEADS * B_NOPE)
    uv2 = w_uv.reshape(B_KV_LORA, B_HEADS * B_VDIM)
    uv_pad = _pad_heads(uv2, B_HEADS, B_VDIM).astype(BF16)
    wo_pad = jnp.pad(w_o.reshape(B_HEADS, B_VDIM, D_MODEL), ((0, 0), (0, HEAD_PAD - B_VDIM), (0, 0)))
    return {
        "down": jnp.pad(w_down, ((0, 0), (0, down_pad))).astype(BF16),
        "g_q": g_q.reshape(1, B_Q_LORA), "g_kv": g_kv.reshape(1, B_KV_LORA),
        "uq": _pad_heads(w_uq, B_HEADS, B_NOPE + B_ROPE).astype(BF16),
        "uk": _pad_heads(uk2, B_HEADS, B_NOPE).astype(BF16),
        "rope_place": place.astype(BF16),
        "uv_t": uv_pad.T,
        "uk_t": jnp.pad(jnp.transpose(w_uk, (1, 2, 0)), ((0, 0), (0, HEAD_PAD - B_NOPE), (0, 0))).astype(BF16),
        "uv_h": jnp.transpose(uv_pad.reshape(B_KV_LORA, B_HEADS, HEAD_PAD), (1, 0, 2)),
        "o": wo_pad.reshape(B_HEADS * HEAD_PAD, D_MODEL).astype(BF16),
    }


def _mlstm_layer(xp, xs, g_pre, g_post, w_in, b_gate, g_head, w_out, state, batch, seq):
    hk, hv = A_HEADS * A_DK, A_HEADS * A_DV
    w_qkv = w_in[:, :2 * hk + hv].astype(BF16)
    w_o = w_in[:, 2 * hk + hv:2 * hk + 2 * hv].astype(BF16)
    w_g = jnp.pad(w_in[:, 2 * hk + 2 * hv:], ((0, 0), (0, LANES - 2 * A_HEADS))).astype(BF16)
    scale = jnp.concatenate([jnp.ones((hk,), F32), jnp.full((hk,), A_DK ** -0.5, F32), jnp.ones((hv,), F32)])
    w_out_b = w_out.astype(BF16)

    (qkv,), (og,), (gt,) = norm_linear(xp, g_pre, [(w_qkv, (BF16,), scale), (w_o, (F32,), None), (w_g, (F32,), None)])
    y, c_p, n_p, m_p = mlstm_prompt(qkv, og, gt, b_gate, g_head, batch, seq)
    xp = linear_norm_res(y, w_out_b, g_post, xp)

    (qkv_s,), (og_s,), (gt_s,) = norm_linear(
        xs, g_pre, [(w_qkv, (F32,), scale), (w_o, (F32,), None), (w_g, (F32,), None)])
    y_s, c_s, n_s, m_s = mlstm_decode(qkv_s[:, :hk], qkv_s[:, hk:2 * hk], qkv_s[:, 2 * hk:], og_s, gt_s,
                                      b_gate, g_head, *state)
    xs = linear_norm_res(y_s, w_out_b, g_post, xs)
    return xp, xs, (c_p, n_p, m_p), (c_s, n_s, m_s)


def _mla_layer(xp, xs, g_pre, g_post, w, pool_lat, pool_kr, layer, page_table, batch, seq):
    nb, n_pages = page_table.shape
    past_len = n_pages * pool_lat.shape[2]
    t = _tile(seq, ATT_TILE)
    tabs_p = _rope_tables(jnp.arange(seq, dtype=jnp.int32))
    ckv_p, kr_p, q, kf, vt = mla_proj(xp, g_pre, w, tabs_p, seq // t, True, BF16, t)
    o = mla_attn(q, kf, vt, batch, seq, t)
    xp = linear_norm_res(o, w["o"], g_post, xp)

    tabs_s = _rope_tables(jnp.full((nb,), past_len, jnp.int32))
    ckv_s, kr_s, q_s = mla_proj(xs, g_pre, w, tabs_s, 1, False, BF16, nb)
    q_lat = head_matmul(q_s, w["uk_t"], BF16).reshape(nb, B_HEADS, B_KV_LORA)
    q_rope = q_s.reshape(nb, B_HEADS, HEAD_PAD)[:, :, B_NOPE:B_NOPE + B_ROPE]
    o_lat = mla_decode(q_lat, q_rope, ckv_s, kr_s, pool_lat, pool_kr, layer, page_table)
    o_s = head_matmul(o_lat.reshape(nb, B_HEADS * B_KV_LORA), w["uv_h"], BF16)
    xs = linear_norm_res(o_s, w["o"], g_post, xs)
    return xp, xs, (ckv_p, kr_p), (ckv_s, kr_s)


def _sb_layer(xp, xs, g_pre, g_post, w_qkv, w_o, pool_k, pool_v, layer, page_table, batch, seq):
    hd = C_HEADS * C_HEAD_DIM
    w_q = w_qkv[:, :hd].astype(BF16)
    w_k = w_qkv[:, hd:2 * hd].astype(BF16)
    w_v = w_qkv[:, 2 * hd:].astype(BF16)
    w_o_b = w_o.astype(BF16)
    (q,), (k32, k16), (v32, v16) = norm_linear(
        xp, g_pre, [(w_q, (BF16,), None), (w_k, (F32, BF16), None), (w_v, (F32, BF16), None)])
    o = sb_attn(q, k16, v16, batch, seq)
    xp = linear_norm_res(o, w_o_b, g_post, xp)

    (q_s,), (k_s,), (v_s,) = norm_linear(xs, g_pre, [(w_q, (F32,), None), (w_k, (F32,), None), (w_v, (F32,), None)])
    o_s = sb_decode(q_s, k_s, v_s, pool_k, pool_v, layer, page_table)
    xs = linear_norm_res(o_s, w_o_b, g_post, xs)
    return xp, xs, (k32, v32), (k_s, v_s)


def kernel(x_prompt, x_sample, state_mlstm_C, state_mlstm_n, state_mlstm_m, cache_mla_latent, cache_mla_krope, cache_sb_k, cache_sb_v, page_table, g_mix_pre, g_mix_post, g_ffn_pre, g_ffn_post, w_ffn_in, w_ffn_out, w_a_in, b_a_gate, g_a_head, w_a_out, w_b_down, g_b_q, g_b_kv, w_b_uq, w_b_uk, w_b_uv, w_b_o, w_c_qkv, w_c_o):
    batch, seq, d = x_prompt.shape
    nb = x_sample.shape[0]
    depth = g_mix_pre.shape[0]
    xp = x_prompt.reshape(batch * seq, d)
    xs = x_sample.reshape(nb, d)
    a_p, a_s, b_p, b_s, c_p, c_s = [], [], [], [], [], []
    for i in range(depth):
        kind, j = i % N_MIXERS, i // N_MIXERS
        if kind == 0:
            state = (state_mlstm_C[j], state_mlstm_n[j], state_mlstm_m[j])
            xp, xs, st_p, st_s = _mlstm_layer(xp, xs, g_mix_pre[i], g_mix_post[i], w_a_in[j], b_a_gate[j],
                                              g_a_head[j], w_a_out[j], state, batch, seq)
            a_p.append(st_p)
            a_s.append(st_s)
        elif kind == 1:
            w = _mla_weights(w_b_down[j], g_b_q[j], g_b_kv[j], w_b_uq[j], w_b_uk[j], w_b_uv[j], w_b_o[j])
            xp, xs, st_p, st_s = _mla_layer(xp, xs, g_mix_pre[i], g_mix_post[i], w, cache_mla_latent,
                                            cache_mla_krope, j, page_table, batch, seq)
            b_p.append(st_p)
            b_s.append(st_s)
        else:
            xp, xs, st_p, st_s = _sb_layer(xp, xs, g_mix_pre[i], g_mix_post[i], w_c_qkv[j], w_c_o[j],
                                           cache_sb_k, cache_sb_v, j, page_table, batch, seq)
            c_p.append(st_p)
            c_s.append(st_s)
        w_gate = w_ffn_in[i][:, :D_FF].astype(BF16)
        w_up = w_ffn_in[i][:, D_FF:].astype(BF16)
        w_out = w_ffn_out[i].astype(BF16)
        xp = ffn(xp, g_ffn_pre[i], w_gate, w_up, w_out, g_ffn_post[i])
        xs = ffn(xs, g_ffn_pre[i], w_gate, w_up, w_out, g_ffn_post[i])

    def stack(states, idx, shape):
        return jnp.stack([s[idx].reshape(shape) for s in states])

    return (xp.reshape(batch, seq, d), xs.reshape(nb, 1, d),
            stack(a_p, 0, (batch, A_HEADS, A_DK, A_DV)), stack(a_p, 1, (batch, A_HEADS, A_DK)),
            stack(a_p, 2, (batch, A_HEADS)),
            stack(a_s, 0, (nb, A_HEADS, A_DK, A_DV)), stack(a_s, 1, (nb, A_HEADS, A_DK)),
            stack(a_s, 2, (nb, A_HEADS)),
            stack(b_p, 0, (batch, seq, B_KV_LORA)), stack(b_p, 1, (batch, seq, B_ROPE)),
            stack(b_s, 0, (nb, 1, B_KV_LORA)), stack(b_s, 1, (nb, 1, B_ROPE)),
            stack(c_p, 0, (batch, seq, C_HEADS, C_HEAD_DIM)), stack(c_p, 1, (batch, seq, C_HEADS, C_HEAD_DIM)),
            stack(c_s, 0, (nb, 1, C_HEADS, C_HEAD_DIM)), stack(c_s, 1, (nb, 1, C_HEADS, C_HEAD_DIM)))
```

```python
import functools

import jax
import jax.numpy as jnp
from jax import lax
from jax.experimental import pallas as pl
from jax.experimental.pallas import tpu as pltpu

F32 = jnp.float32
BF16 = jnp.bfloat16

D_MODEL = 1024
N_MIXERS = 3
A_HEADS = 4
A_DK = 128
A_DV = 256
M_INIT = -1e30
B_HEADS = 16
B_NOPE = 64
B_ROPE = 32
B_VDIM = 64
B_Q_LORA = 768
B_KV_LORA = 256
ROPE_THETA = 10000.0
MLA_SCALE = (B_NOPE + B_ROPE) ** -0.5
C_HEADS = 16
C_HEAD_DIM = 64
C_SCALE = C_HEAD_DIM ** -0.5
D_FF = 2816
EPS = 1e-6

LANES = 128
HEAD_PAD = LANES
FFN_CHUNK = 256
NL_CHUNK = 512
VMEM_LIMIT = 56 * 1024 * 1024
ROW_TILE = 512
MLSTM_CHUNK = 256
ATT_TILE = 512
SB_TILE = 256
MLA_DEC_PAGES = 8
ATT_SPLIT = 2
NEG_INF = float("-inf")
SB_STOP = -105.0
LOG2_E = 1.4426950408889634


def _params(*sem):
    return pltpu.CompilerParams(dimension_semantics=sem, vmem_limit_bytes=VMEM_LIMIT)


def _rms(x, g):
    return x * lax.rsqrt(jnp.mean(x * x, axis=-1, keepdims=True) + EPS) * g


def _log_sigmoid(x):
    return -(jnp.maximum(-x, 0.0) + jnp.log1p(jnp.exp(-jnp.abs(x))))


def _neg_softplus(z):
    return jnp.minimum(-z, 0.0) - jnp.log(1.0 + jnp.exp(-jnp.abs(z)))


def _dot(a, b):
    return jnp.dot(a, b, preferred_element_type=F32)


def _dot_nt(a, b):
    return lax.dot_general(a, b, (((1,), (1,)), ((), ())), preferred_element_type=F32)


def _tile(n, t):
    t = min(n, t)
    assert n % t == 0, (n, t)
    return t


def _norm_linear_kernel(x_ref, g_ref, *refs, groups):
    h = _rms(x_ref[...], g_ref[...]).astype(BF16)
    n_in = sum(1 + has_scale for _, _, has_scale in groups)
    ins, outs = refs[:n_in], refs[n_in:]
    ii = oi = 0
    for n, transposed, has_scale in groups:
        w_ref = ins[ii]
        s_ref = ins[ii + 1] if has_scale else None
        ii += 1 + has_scale
        for c0 in range(0, n, NL_CHUNK):
            cols = slice(c0, min(n, c0 + NL_CHUNK))
            acc = _dot(h, w_ref[:, cols])
            if has_scale:
                acc = acc * s_ref[:, cols]
            for o, tr in zip(outs[oi:oi + len(transposed)], transposed):
                if tr:
                    o[cols, :] = acc.T.astype(o.dtype)
                else:
                    o[:, cols] = acc.astype(o.dtype)
        oi += len(transposed)


def norm_linear(x, g, groups, seq=None):
    m, k = x.shape
    tm = _tile(m, ROW_TILE)
    const = lambda i: (0, 0)
    row = lambda i: (i, 0)
    in_specs = [pl.BlockSpec((tm, k), row), pl.BlockSpec((1, k), const)]
    args = [x, g.reshape(1, k)]
    out_specs, out_shape, meta = [], [], []
    for w, out_dtypes, col_scale in groups:
        n = w.shape[1]
        in_specs.append(pl.BlockSpec((k, n), const, pipeline_mode=pl.Buffered(1)))
        args.append(w)
        if col_scale is not None:
            in_specs.append(pl.BlockSpec((1, n), const))
            args.append(col_scale.reshape(1, n))
        transposed = []
        for dt in out_dtypes:
            if isinstance(dt, tuple):
                tiles = seq // tm
                out_specs.append(pl.BlockSpec((None, n, tm), lambda i: (i // tiles, 0, i % tiles)))
                out_shape.append(jax.ShapeDtypeStruct((m // seq, n, seq), dt[1]))
                transposed.append(True)
            else:
                out_specs.append(pl.BlockSpec((tm, n), row))
                out_shape.append(jax.ShapeDtypeStruct((m, n), dt))
                transposed.append(False)
        meta.append((n, tuple(transposed), col_scale is not None))
    outs = pl.pallas_call(
        functools.partial(_norm_linear_kernel, groups=tuple(meta)),
        grid=(m // tm,),
        in_specs=in_specs, out_specs=out_specs, out_shape=out_shape,
        compiler_params=_params("parallel"),
        name="norm_linear",
    )(*args)
    res, oi = [], 0
    for _, transposed, _ in meta:
        res.append(list(outs[oi:oi + len(transposed)]))
        oi += len(transposed)
    return res


def _linear_norm_res_kernel(y_ref, w_ref, g_ref, r_ref, o_ref):
    acc = _dot(y_ref[...].astype(BF16), w_ref[...])
    o_ref[...] = r_ref[...] + _rms(acc, g_ref[...])


def linear_norm_res(y, w, g, res):
    m, k = y.shape
    n = w.shape[1]
    tm = _tile(m, ROW_TILE)
    return pl.pallas_call(
        _linear_norm_res_kernel,
        grid=(m // tm,),
        in_specs=[pl.BlockSpec((tm, k), lambda i: (i, 0)),
                  pl.BlockSpec((k, n), lambda i: (0, 0)),
                  pl.BlockSpec((1, n), lambda i: (0, 0)),
                  pl.BlockSpec((tm, n), lambda i: (i, 0))],
        out_specs=pl.BlockSpec((tm, n), lambda i: (i, 0)),
        out_shape=jax.ShapeDtypeStruct((m, n), F32),
        compiler_params=_params("parallel"),
        name="linear_norm_res",
    )(y, w, g.reshape(1, n), res)


def _ffn_kernel(x_ref, gp_ref, wg_ref, wu_ref, wo_ref, gq_ref, o_ref):
    x = x_ref[...]
    h = _rms(x, gp_ref[...]).astype(BF16)
    acc = jnp.zeros(x.shape, F32)
    for c in range(D_FF // FFN_CHUNK):
        sl = slice(c * FFN_CHUNK, (c + 1) * FFN_CHUNK)
        a = _dot(h, wg_ref[:, sl])
        u = _dot(h, wu_ref[:, sl])
        act = (a * jax.nn.sigmoid(a) * u).astype(BF16)
        acc = acc + _dot(act, wo_ref[sl, :])
    o_ref[...] = x + _rms(acc, gq_ref[...])


def ffn(x, g_pre, w_gate, w_up, w_out, g_post):
    m, d = x.shape
    tm = _tile(m, ROW_TILE)
    const = lambda i: (0, 0)
    return pl.pallas_call(
        _ffn_kernel,
        grid=(m // tm,),
        in_specs=[pl.BlockSpec((tm, d), lambda i: (i, 0)),
                  pl.BlockSpec((1, d), const),
                  pl.BlockSpec((d, D_FF), const, pipeline_mode=pl.Buffered(1)),
                  pl.BlockSpec((d, D_FF), const, pipeline_mode=pl.Buffered(1)),
                  pl.BlockSpec((D_FF, d), const, pipeline_mode=pl.Buffered(1)),
                  pl.BlockSpec((1, d), const)],
        out_specs=pl.BlockSpec((tm, d), lambda i: (i, 0)),
        out_shape=jax.ShapeDtypeStruct((m, d), F32),
        compiler_params=_params("parallel"),
        name="ffn",
    )(x, g_pre.reshape(1, d), w_gate, w_up, w_out, g_post.reshape(1, d))


def _mlstm_prompt_kernel(q_ref, k_ref, v_ref, o_ref, gt_ref, bg_ref, gh_ref,
                         y_ref, c_ref, n_ref, m_ref, *, chunk):
    L = chunk

    @pl.when(pl.program_id(1) == 0)
    def _():
        c_ref[...] = jnp.zeros(c_ref.shape, F32)
        n_ref[...] = jnp.zeros(n_ref.shape, F32)
        m_ref[...] = jnp.full(m_ref.shape, M_INIT, F32)

    gates = gt_ref[...] + bg_ref[...]
    lane = lax.broadcasted_iota(jnp.int32, (L, LANES), 1)
    log_f = jnp.where(lane >= A_HEADS, _log_sigmoid(gates), 0.0)
    row = lax.broadcasted_iota(jnp.int32, (L, L), 0)
    col = lax.broadcasted_iota(jnp.int32, (L, L), 1)
    causal = col <= row
    tri = jnp.where(causal, 1.0, 0.0).astype(F32)
    b_all = jnp.dot(tri, log_f, precision=lax.Precision.HIGHEST, preferred_element_type=F32)
    packed = jnp.where(lane < A_HEADS, gates, b_all)
    packed_t = packed.T

    for h in range(A_HEADS):
        li_col = packed[:, h:h + 1]
        b_col = packed[:, A_HEADS + h:A_HEADS + h + 1]
        li_row = packed_t[h:h + 1, :]
        b_row = packed_t[A_HEADS + h:A_HEADS + h + 1, :]
        q = q_ref[:, h * A_DK:(h + 1) * A_DK]
        k = k_ref[:, h * A_DK:(h + 1) * A_DK]
        v = v_ref[:, h * A_DV:(h + 1) * A_DV]
        m_prev = m_ref[0, :, h:h + 1]
        c_mem = c_ref[0, h]
        n_row = n_ref[0, h:h + 1, :]

        log_d = jnp.where(causal, b_col - b_row + li_row, NEG_INF)
        log_inter = b_col + m_prev
        m_t = jnp.maximum(log_inter, jnp.max(log_d, axis=1, keepdims=True))
        s = _dot_nt(q, k) * jnp.exp(log_d - m_t)
        w_inter = jnp.exp(log_inter - m_t)
        num = w_inter * _dot(q, c_mem.astype(BF16)) + _dot(s.astype(BF16), v)
        qn = jnp.sum(q.astype(F32) * n_row, axis=1, keepdims=True)
        den = w_inter * qn + jnp.sum(s, axis=1, keepdims=True)
        hh = num / jnp.maximum(jnp.abs(den), jnp.exp(-m_t))
        hn = _rms(hh, gh_ref[:, h * A_DV:(h + 1) * A_DV])
        gate = jax.nn.sigmoid(o_ref[:, h * A_DV:(h + 1) * A_DV])
        y_ref[:, h * A_DV:(h + 1) * A_DV] = (gate * hn).astype(y_ref.dtype)

        m_new = m_t[L - 1:L, :]
        b_last = b_col[L - 1:L, :]
        w_col = jnp.exp(b_last - b_col + li_col - m_new)
        decay = jnp.exp(b_last + m_prev - m_new)
        kw = k.astype(F32) * w_col
        c_ref[0, h] = decay * c_mem + _dot(kw.T.astype(BF16), v)
        n_ref[0, h:h + 1, :] = decay * n_row + jnp.sum(kw, axis=0, keepdims=True)
        m_ref[0, :, h:h + 1] = m_new


def mlstm_prompt(qkv, o_gate, gates, b_gate, g_head, batch, seq):
    L = _tile(seq, MLSTM_CHUNK)
    nc = seq // L
    hk = A_HEADS * A_DK
    hv = A_HEADS * A_DV
    m = batch * seq
    row_map = lambda b, c: (b * nc + c, 0)
    const = lambda b, c: (0, 0)
    bg = jnp.zeros((1, LANES), F32).at[0, :2 * A_HEADS].set(b_gate)
    y, c_fin, n_fin, m_fin = pl.pallas_call(
        functools.partial(_mlstm_prompt_kernel, chunk=L),
        grid=(batch, nc),
        in_specs=[pl.BlockSpec((L, hk), row_map),
                  pl.BlockSpec((L, hk), lambda b, c: (b * nc + c, 1)),
                  pl.BlockSpec((L, hv), lambda b, c: (b * nc + c, 1)),
                  pl.BlockSpec((L, hv), row_map),
                  pl.BlockSpec((L, LANES), row_map),
                  pl.BlockSpec((1, LANES), const),
                  pl.BlockSpec((1, hv), const)],
        out_specs=[pl.BlockSpec((L, hv), row_map),
                   pl.BlockSpec((1, A_HEADS, A_DK, A_DV), lambda b, c: (b, 0, 0, 0)),
                   pl.BlockSpec((1, A_HEADS, A_DK), lambda b, c: (b, 0, 0)),
                   pl.BlockSpec((1, 1, LANES), lambda b, c: (b, 0, 0))],
        out_shape=[jax.ShapeDtypeStruct((m, hv), BF16),
                   jax.ShapeDtypeStruct((batch, A_HEADS, A_DK, A_DV), F32),
                   jax.ShapeDtypeStruct((batch, A_HEADS, A_DK), F32),
                   jax.ShapeDtypeStruct((batch, 1, LANES), F32)],
        compiler_params=_params("parallel", "arbitrary"),
        name="mlstm_prompt",
    )(qkv, qkv, qkv, o_gate, gates, bg, g_head.reshape(1, hv))
    return y, c_fin, n_fin, m_fin[:, 0, :A_HEADS]


def _mlstm_decode_kernel(q_ref, k_ref, v_ref, o_ref, gt_ref, bg_ref, gh_ref, c_ref, n_ref, m_ref,
                         y_ref, co_ref, no_ref, mo_ref, *, tb):
    r = lax.broadcasted_iota(jnp.int32, (A_DK, A_DK), 0)
    c = lax.broadcasted_iota(jnp.int32, (A_DK, A_DK), 1)
    eye = r == c
    lane = lax.broadcasted_iota(jnp.int32, (1, LANES), 1)

    def to_col(x_row):
        return jnp.sum(jnp.where(eye, jnp.broadcast_to(x_row, (A_DK, A_DK)), 0.0), axis=1, keepdims=True)

    def body(bb, carry):
        gates = gt_ref[bb] + bg_ref[...]
        log_f = _log_sigmoid(gates)
        m_row = m_ref[bb]
        m_out = jnp.zeros((1, LANES), F32)
        q_all = q_ref[bb]
        k_all = k_ref[bb]
        v_all = v_ref[bb]
        o_all = o_ref[bb]
        for h in range(A_HEADS):
            li = gates[:, h:h + 1]
            lf = log_f[:, A_HEADS + h:A_HEADS + h + 1]
            m_prev = m_row[:, h:h + 1]
            m_t = jnp.maximum(lf + m_prev, li)
            w_inter = jnp.exp(lf + m_prev - m_t)
            w_new = jnp.exp(li - m_t)
            q = q_all[:, h * A_DK:(h + 1) * A_DK]
            k = k_all[:, h * A_DK:(h + 1) * A_DK]
            v = v_all[:, h * A_DV:(h + 1) * A_DV]
            c_mem = c_ref[bb, h]
            n_row = n_ref[bb, pl.ds(h, 1), :]
            s = jnp.sum(q * k, axis=1, keepdims=True) * w_new
            qc = jnp.sum(c_mem * to_col(q), axis=0, keepdims=True)
            num = w_inter * qc + s * v
            den = w_inter * jnp.sum(q * n_row, axis=1, keepdims=True) + s
            hh = num / jnp.maximum(jnp.abs(den), jnp.exp(-m_t))
            hn = _rms(hh, gh_ref[:, h * A_DV:(h + 1) * A_DV])
            gate = jax.nn.sigmoid(o_all[:, h * A_DV:(h + 1) * A_DV])
            y_ref[bb, :, h * A_DV:(h + 1) * A_DV] = gate * hn
            kw = k * w_new
            co_ref[bb, h] = w_inter * c_mem + to_col(kw) * v
            no_ref[bb, pl.ds(h, 1), :] = w_inter * n_row + kw
            m_out = jnp.where(lane == h, m_t, m_out)
        mo_ref[bb] = m_out
        return carry

    lax.fori_loop(0, tb, body, 0)


def mlstm_decode(q, k, v, o_gate, gates, b_gate, g_head, c_all, n_all, m0, layer):
    nb = q.shape[0]
    tb = _tile(nb, 8)
    hk = A_HEADS * A_DK
    hv = A_HEADS * A_DV
    bg = jnp.zeros((1, LANES), F32).at[0, :2 * A_HEADS].set(b_gate)
    m_pad = jnp.zeros((nb, 1, LANES), F32).at[:, 0, :A_HEADS].set(m0)
    row3 = lambda i: (i, 0, 0)
    const = lambda i: (0, 0)
    y, c1, n1, m1 = pl.pallas_call(
        functools.partial(_mlstm_decode_kernel, tb=tb),
        grid=(nb // tb,),
        in_specs=[pl.BlockSpec((tb, 1, hk), row3),
                  pl.BlockSpec((tb, 1, hk), row3),
                  pl.BlockSpec((tb, 1, hv), row3),
                  pl.BlockSpec((tb, 1, hv), row3),
                  pl.BlockSpec((tb, 1, LANES), row3),
                  pl.BlockSpec((1, LANES), const),
                  pl.BlockSpec((1, hv), const),
                  pl.BlockSpec((None, tb, A_HEADS, A_DK, A_DV), lambda i: (layer, i, 0, 0, 0)),
                  pl.BlockSpec((None, tb, A_HEADS, A_DK), lambda i: (layer, i, 0, 0)),
                  pl.BlockSpec((tb, 1, LANES), row3)],
        out_specs=[pl.BlockSpec((tb, 1, hv), row3),
                   pl.BlockSpec((tb, A_HEADS, A_DK, A_DV), lambda i: (i, 0, 0, 0)),
                   pl.BlockSpec((tb, A_HEADS, A_DK), row3),
                   pl.BlockSpec((tb, 1, LANES), row3)],
        out_shape=[jax.ShapeDtypeStruct((nb, 1, hv), F32),
                   jax.ShapeDtypeStruct((nb, A_HEADS, A_DK, A_DV), F32),
                   jax.ShapeDtypeStruct((nb, A_HEADS, A_DK), F32),
                   jax.ShapeDtypeStruct((nb, 1, LANES), F32)],
        compiler_params=_params("parallel"),
        name="mlstm_decode",
    )(q.reshape(nb, 1, hk), k.reshape(nb, 1, hk), v.reshape(nb, 1, hv), o_gate.reshape(nb, 1, hv),
      gates.reshape(nb, 1, LANES), bg, g_head.reshape(1, hv), c_all, n_all, m_pad)
    return y.reshape(nb, hv), c1, n1, m1[:, 0, :A_HEADS]


def _rope_tables(pos):
    half = B_ROPE // 2
    inv_freq = ROPE_THETA ** (-jnp.arange(half, dtype=F32) / half)
    ang = pos.astype(F32)[:, None] * inv_freq[None, :]
    cos, sin = jnp.cos(ang), jnp.sin(ang)
    cos_k = jnp.concatenate([cos, cos], axis=1)
    sin_k = jnp.concatenate([-sin, sin], axis=1)
    ones = jnp.ones((pos.shape[0], B_NOPE), F32)
    tail = HEAD_PAD - B_NOPE - B_ROPE
    cos_q = jnp.concatenate([ones, cos_k, jnp.ones((pos.shape[0], tail), F32)], axis=1)
    sin_q = jnp.concatenate([0 * ones, sin_k, jnp.zeros((pos.shape[0], tail), F32)], axis=1)
    return cos_k, sin_k, cos_q, sin_q


def _mla_proj_kernel(*refs, want_kv):
    (x_ref, gp_ref, wd_ref, gq_ref, gkv_ref, ck_ref, sk_ref, cq_ref, sq_ref, wuq_ref) = refs[:10]
    if want_kv:
        wk_ref, we_ref, wv_ref = refs[10:13]
        ckv_ref, kr_ref, q_ref, kf_ref, v_ref = refs[13:]
    else:
        ckv_ref, kr_ref, q_ref = refs[10:]
    half = B_ROPE // 2
    h = _rms(x_ref[...], gp_ref[...]).astype(BF16)
    d = _dot(h, wd_ref[...])
    c_q = _rms(d[:, :B_Q_LORA], gq_ref[...]).astype(BF16)
    c_kv = _rms(d[:, B_Q_LORA:B_Q_LORA + B_KV_LORA], gkv_ref[...])
    ckv_ref[...] = c_kv
    kr = d[:, B_Q_LORA + B_KV_LORA:B_Q_LORA + B_KV_LORA + B_ROPE]
    kr_rot = jnp.concatenate([kr[:, half:], kr[:, :half]], axis=1)
    k_rope = kr * ck_ref[...] + kr_rot * sk_ref[...]
    kr_ref[...] = k_rope

    q = _dot(c_q, wuq_ref[...])
    lane = lax.broadcasted_iota(jnp.int32, (q.shape[0], HEAD_PAD), 1)
    first = lane < B_NOPE + half
    cq, sq = cq_ref[...], sq_ref[...]
    for hd in range(B_HEADS):
        xg = q[:, hd * HEAD_PAD:(hd + 1) * HEAD_PAD]
        rot = jnp.where(first, pltpu.roll(xg, HEAD_PAD - half, 1), pltpu.roll(xg, half, 1))
        q_ref[:, hd * HEAD_PAD:(hd + 1) * HEAD_PAD] = (xg * cq + rot * sq).astype(q_ref.dtype)

    if want_kv:
        c_kv_b = c_kv.astype(BF16)
        kr_pad = jnp.concatenate([k_rope, jnp.zeros((k_rope.shape[0], LANES - B_ROPE), F32)], axis=1).astype(BF16)
        kf_ref[...] = (_dot(c_kv_b, wk_ref[...]) + _dot(kr_pad, we_ref[...])).astype(BF16)
        v_ref[0] = _dot_nt(wv_ref[...], c_kv_b).astype(BF16)


def mla_proj(x, g_pre, w, tabs, n_pos_blocks, want_kv, q_dtype, tm):
    m, d = x.shape
    hp = B_HEADS * HEAD_PAD
    cos_k, sin_k, cos_q, sin_q = tabs
    assert cos_k.shape[0] == n_pos_blocks * tm
    const = lambda i: (0, 0)
    row = lambda i: (i, 0)
    posm = lambda i: (i % n_pos_blocks, 0)
    in_specs = [pl.BlockSpec((tm, d), row),
                pl.BlockSpec((1, d), const),
                pl.BlockSpec(w["down"].shape, const),
                pl.BlockSpec((1, B_Q_LORA), const),
                pl.BlockSpec((1, B_KV_LORA), const),
                pl.BlockSpec((tm, B_ROPE), posm),
                pl.BlockSpec((tm, B_ROPE), posm),
                pl.BlockSpec((tm, HEAD_PAD), posm),
                pl.BlockSpec((tm, HEAD_PAD), posm),
                pl.BlockSpec(w["uq"].shape, const)]
    args = [x, g_pre.reshape(1, d), w["down"], w["g_q"], w["g_kv"], cos_k, sin_k, cos_q, sin_q, w["uq"]]
    out_specs = [pl.BlockSpec((tm, B_KV_LORA), row),
                 pl.BlockSpec((tm, B_ROPE), row),
                 pl.BlockSpec((tm, hp), row)]
    out_shape = [jax.ShapeDtypeStruct((m, B_KV_LORA), F32),
                 jax.ShapeDtypeStruct((m, B_ROPE), F32),
                 jax.ShapeDtypeStruct((m, hp), q_dtype)]
    if want_kv:
        in_specs += [pl.BlockSpec(w["uk"].shape, const),
                     pl.BlockSpec(w["rope_place"].shape, const),
                     pl.BlockSpec(w["uv_t"].shape, const)]
        args += [w["uk"], w["rope_place"], w["uv_t"]]
        out_specs += [pl.BlockSpec((tm, hp), row), pl.BlockSpec((1, hp, tm), lambda i: (i, 0, 0))]
        out_shape += [jax.ShapeDtypeStruct((m, hp), BF16), jax.ShapeDtypeStruct((m // tm, hp, tm), BF16)]
    return pl.pallas_call(
        functools.partial(_mla_proj_kernel, want_kv=want_kv),
        grid=(m // tm,),
        in_specs=in_specs, out_specs=out_specs, out_shape=out_shape,
        compiler_params=_params("parallel"),
        name="mla_proj",
    )(*args)


def _tree_reduce(x, op):
    rows, w = x.shape
    x = x.reshape(rows // 8, 8, w)
    while x.shape[0] > 1:
        n = x.shape[0] // 2
        x = op(x[:n], x[n:])
    red = jnp.max if op is jnp.maximum else jnp.sum
    return red(x[0], axis=0, keepdims=True)


def _mla_attn_kernel(q_ref, k_ref, vt_ref, o_ref, s0_scr, s1_scr, m_scr, l_scr, acc_scr, *, t):
    i = pl.program_id(2)
    q_t = q_ref[...].astype(F32).T.astype(BF16)
    m_scr[...] = jnp.full(m_scr.shape, NEG_INF, F32)
    l_scr[...] = jnp.zeros(l_scr.shape, F32)
    acc_scr[...] = jnp.zeros(acc_scr.shape, F32)
    w = t // ATT_SPLIT
    key = lax.broadcasted_iota(jnp.int32, (t, w), 0)
    qry = lax.broadcasted_iota(jnp.int32, (t, w), 1)
    log2_scale = MLA_SCALE * LOG2_E

    def scores(j, s_scr):
        start = pl.multiple_of(j * t, t)
        s_scr[...] = _dot(k_ref[pl.ds(start, t), :], q_t)

    def update(j, s_scr, masked):
        vtb = vt_ref[j]
        for c in range(ATT_SPLIT):
            cols = slice(c * w, (c + 1) * w)
            s_t = s_scr[:, cols]
            if masked:
                s_t = jnp.where(key <= qry + c * w, s_t, NEG_INF)
            m_old = m_scr[:, cols]
            m_new = jnp.maximum(m_old, _tree_reduce(s_t, jnp.maximum))
            alpha = jnp.exp2((m_old - m_new) * log2_scale)
            p = jnp.exp2((s_t - m_new) * log2_scale)
            l_scr[:, cols] = alpha * l_scr[:, cols] + _tree_reduce(p, jnp.add)
            acc_scr[:, cols] = alpha * acc_scr[:, cols] + _dot(vtb, p.astype(BF16))
            m_scr[:, cols] = m_new

    scores(0, s0_scr)

    def body(r, carry):
        j = 2 * r
        scores(j + 1, s1_scr)
        update(j, s0_scr, False)
        scores(j + 2, s0_scr)
        update(j + 1, s1_scr, False)
        return carry

    lax.fori_loop(0, i // 2, body, 0)

    @pl.when(i % 2 == 0)
    def _():
        update(i, s0_scr, True)

    @pl.when(i % 2 == 1)
    def _():
        scores(i, s1_scr)
        update(i - 1, s0_scr, False)
        update(i, s1_scr, True)

    o_ref[...] = (acc_scr[...] / l_scr[...]).T.astype(o_ref.dtype)


def mla_attn(q, kf, vt, batch, seq, t):
    nq = seq // t
    m = batch * seq
    hp = B_HEADS * HEAD_PAD
    return pl.pallas_call(
        functools.partial(_mla_attn_kernel, t=t),
        grid=(batch, B_HEADS, nq),
        in_specs=[pl.BlockSpec((t, HEAD_PAD), lambda b, h, i: (b * nq + i, h)),
                  pl.BlockSpec((seq, HEAD_PAD), lambda b, h, i: (b, h)),
                  pl.BlockSpec((nq, HEAD_PAD, t), lambda b, h, i: (b, h, 0))],
        out_specs=pl.BlockSpec((t, HEAD_PAD), lambda b, h, i: (b * nq + i, h)),
        out_shape=jax.ShapeDtypeStruct((m, hp), BF16),
        scratch_shapes=[pltpu.VMEM((t, t), F32), pltpu.VMEM((t, t), F32), pltpu.VMEM((1, t), F32), pltpu.VMEM((1, t), F32),
                        pltpu.VMEM((HEAD_PAD, t), F32)],
        compiler_params=_params("parallel", "parallel", "arbitrary"),
        name="mla_attn",
    )(q, kf, vt)


def _head_matmul_kernel(x_ref, w_ref, o_ref):
    o_ref[...] = _dot(x_ref[...].astype(BF16), w_ref[0]).astype(o_ref.dtype)


def head_matmul(x, w, out_dtype):
    nb = x.shape[0]
    nh, k, n = w.shape
    return pl.pallas_call(
        _head_matmul_kernel,
        grid=(nh,),
        in_specs=[pl.BlockSpec((nb, k), lambda h: (0, h)),
                  pl.BlockSpec((1, k, n), lambda h: (h, 0, 0))],
        out_specs=pl.BlockSpec((nb, n), lambda h: (0, h)),
        out_shape=jax.ShapeDtypeStruct((nb, nh * n), out_dtype),
        compiler_params=_params("parallel"),
        name="head_matmul",
    )(x, w)


def _mla_decode_kernel(pt_ref, ql_ref, qr_ref, cn_ref, kn_ref, lat_hbm, kr_hbm, o_ref,
                       lat_buf, kr_buf, s_scr, sem, *, layer, n_pages, n_seq, group):
    n_groups = n_pages // group
    page = lat_buf.shape[2]

    def page_copies(b, slot, g):
        pid = pt_ref[b * n_pages + g]
        return (pltpu.make_async_copy(lat_hbm.at[layer, pid], lat_buf.at[slot, g], sem.at[0, slot]),
                pltpu.make_async_copy(kr_hbm.at[layer, pid], kr_buf.at[slot, g], sem.at[1, slot]))

    def start_seq(b, slot):
        def one(g, carry):
            for cp in page_copies(b, slot, g):
                cp.start()
            return carry

        lax.fori_loop(0, n_pages, one, 0)

    def wait_seq(b, slot):
        def one(g, carry):
            for cp in page_copies(b, slot, g):
                cp.wait()
            return carry

        lax.fori_loop(0, n_pages, one, 0)

    start_seq(0, 0)

    def seq_body(b, carry):
        slot = b % 2

        @pl.when(b + 1 < n_seq)
        def _():
            start_seq(b + 1, 1 - slot)

        wait_seq(b, slot)
        ql = ql_ref[b]
        qr = qr_ref[b]
        cn = cn_ref[b]
        kn = kn_ref[b]
        s_new = (jnp.sum(ql.astype(F32) * cn, axis=1, keepdims=True)
                 + jnp.sum(qr.astype(F32) * kn, axis=1, keepdims=True)) * MLA_SCALE

        def pass_scores(grp, m):
            parts = []
            for gi in range(group):
                g = grp * group + gi
                lat = lat_buf[slot, g].astype(BF16)
                kr = kr_buf[slot, g].astype(BF16)
                parts.append(_dot_nt(ql, lat) + _dot(qr, kr))
            s = jnp.concatenate(parts, axis=1) * MLA_SCALE
            s_scr[grp] = s
            return jnp.maximum(m, jnp.max(s, axis=1, keepdims=True))

        m = lax.fori_loop(0, n_groups, pass_scores, s_new)

        def pass_values(grp, state):
            l, acc = state
            p = jnp.exp(s_scr[grp] - m)
            pb = p.astype(BF16)
            for gi in range(group):
                lat = lat_buf[slot, grp * group + gi].astype(BF16)
                acc = acc + _dot(pb[:, gi * page:(gi + 1) * page], lat)
            return l + jnp.sum(p, axis=1, keepdims=True), acc

        p_new = jnp.exp(s_new - m)
        l, acc = lax.fori_loop(0, n_groups, pass_values, (p_new, p_new * cn))
        o_ref[b] = acc / l
        return carry

    lax.fori_loop(0, n_seq, seq_body, 0)


def mla_decode(q_lat, q_rope, c_new, k_new, pool_lat, pool_kr, layer, page_table):
    nb, n_pages = page_table.shape
    page = pool_lat.shape[2]
    group = _tile(n_pages, MLA_DEC_PAGES)
    full = lambda i, pt: (0, 0, 0)
    grid_spec = pltpu.PrefetchScalarGridSpec(
        num_scalar_prefetch=1,
        grid=(1,),
        in_specs=[pl.BlockSpec((nb, B_HEADS, B_KV_LORA), full),
                  pl.BlockSpec((nb, B_HEADS, B_ROPE), full),
                  pl.BlockSpec((nb, 1, B_KV_LORA), full),
                  pl.BlockSpec((nb, 1, B_ROPE), full),
                  pl.BlockSpec(memory_space=pl.ANY),
                  pl.BlockSpec(memory_space=pl.ANY)],
        out_specs=pl.BlockSpec((nb, B_HEADS, B_KV_LORA), full),
        scratch_shapes=[pltpu.VMEM((2, n_pages, page, B_KV_LORA), F32),
                        pltpu.VMEM((2, n_pages, B_ROPE, page), F32),
                        pltpu.VMEM((n_pages // group, B_HEADS, group * page), F32),
                        pltpu.SemaphoreType.DMA((2, 2))])
    return pl.pallas_call(
        functools.partial(_mla_decode_kernel, layer=layer, n_pages=n_pages, n_seq=nb, group=group),
        grid_spec=grid_spec,
        out_shape=jax.ShapeDtypeStruct((nb, B_HEADS, B_KV_LORA), F32),
        compiler_params=_params("arbitrary"),
        name="mla_decode",
    )(page_table.reshape(-1), q_lat, q_rope, c_new.reshape(nb, 1, B_KV_LORA), k_new.reshape(nb, 1, B_ROPE),
      pool_lat, jnp.transpose(pool_kr, (0, 1, 3, 2)))


def _suffix_sums(log_rest, ones_tri):
    hi = log_rest.astype(BF16)
    lo = (log_rest - hi.astype(F32)).astype(BF16)
    return _dot(hi, ones_tri) + _dot(lo, ones_tri)


def _sb_attn_kernel(q_ref, k_ref, v_ref, o_ref, acc_scr, c_scr, *, t):
    i = pl.program_id(2)
    half = C_HEAD_DIM
    lane = lax.broadcasted_iota(jnp.int32, (t, 2 * half), 1)
    qs = q_ref[...] * jnp.asarray(C_SCALE, BF16)
    zero = jnp.zeros_like(qs)
    q_heads = (jnp.where(lane < half, qs, zero), jnp.where(lane >= half, qs, zero))
    row = lax.broadcasted_iota(jnp.int32, (t, t), 0)
    col = lax.broadcasted_iota(jnp.int32, (t, t), 1)
    ones_tri = jnp.where(row >= col, 1.0, 0.0).astype(BF16)
    valid = col < row
    acc_scr[...] = jnp.zeros(acc_scr.shape, F32)

    def tile(j, masked, c_heads):
        start = pl.multiple_of(j * t, t)
        kb = k_ref[pl.ds(start, t), :]
        vb = v_ref[pl.ds(start, t), :]
        c_out = []
        for hh in range(2):
            z = _dot_nt(q_heads[hh], kb)
            log_rest = _neg_softplus(z)
            if masked:
                log_rest = jnp.where(valid, log_rest, 0.0)
            tail = _suffix_sums(log_rest, ones_tri)
            a = jnp.exp(z + tail + c_heads[hh])
            if masked:
                a = jnp.where(valid, a, 0.0)
            acc_scr[hh] = acc_scr[hh] + _dot(a.astype(BF16), vb)
            c_out.append(c_heads[hh] + tail[:, 0:1])
        return c_out

    none = jnp.zeros((t, 1), F32)
    c_diag = tile(i, True, [none, none])
    gone = jnp.where(i > 0, 0.0, NEG_INF)
    c_prev = tile(jnp.maximum(i - 1, 0), False, [c + gone for c in c_diag])
    c_scr[0] = c_prev[0]
    c_scr[1] = c_prev[1]

    def live():
        return (jnp.max(jnp.maximum(c_scr[0], c_scr[1])) > SB_STOP).astype(jnp.int32)

    def cond(state):
        r, go = state
        return jnp.logical_and(r < i - 1, go > 0)

    def body(state):
        r, _ = state
        c_new = tile(i - 2 - r, False, [c_scr[0], c_scr[1]])
        c_scr[0] = c_new[0]
        c_scr[1] = c_new[1]
        return r + 1, live()

    lax.while_loop(cond, body, (jnp.int32(0), live()))
    o_ref[...] = jnp.where(lane < half, acc_scr[0], acc_scr[1]).astype(o_ref.dtype)


def sb_attn(q, k, v, batch, seq):
    t = _tile(seq, SB_TILE)
    nq = seq // t
    m = batch * seq
    w = 2 * C_HEAD_DIM
    return pl.pallas_call(
        functools.partial(_sb_attn_kernel, t=t),
        grid=(batch, C_HEADS // 2, nq),
        in_specs=[pl.BlockSpec((t, w), lambda b, h, i: (b * nq + i, h)),
                  pl.BlockSpec((seq, w), lambda b, h, i: (b, h)),
                  pl.BlockSpec((seq, w), lambda b, h, i: (b, h))],
        out_specs=pl.BlockSpec((t, w), lambda b, h, i: (b * nq + i, h)),
        out_shape=jax.ShapeDtypeStruct((m, C_HEADS * C_HEAD_DIM), BF16),
        scratch_shapes=[pltpu.VMEM((2, t, w), F32), pltpu.VMEM((2, t, 1), F32)],
        compiler_params=_params("parallel", "parallel", "arbitrary"),
        name="sb_attn",
    )(q, k, v)


def _sb_decode_kernel(pt_ref, q_ref, kn_ref, vn_ref, pk_ref, pv_ref, o_ref,
                      kbuf, vbuf, sem, c_scr, acc_scr, *, layer, n_pages, n_seq, new_valid):
    hd = C_HEADS * C_HEAD_DIM
    page = kbuf.shape[2]
    head_of_lane = lax.broadcasted_iota(jnp.int32, (C_HEADS, hd), 1) // C_HEAD_DIM
    own = head_of_lane == lax.broadcasted_iota(jnp.int32, (C_HEADS, hd), 0)
    row = lax.broadcasted_iota(jnp.int32, (page, page), 0)
    col = lax.broadcasted_iota(jnp.int32, (page, page), 1)
    ones_tri = jnp.where(row >= col, 1.0, 0.0).astype(BF16)

    def page_copies(b, j, slot):
        pid = pt_ref[b * n_pages + n_pages - 1 - j]
        return (pltpu.make_async_copy(pk_ref.at[layer, pid], kbuf.at[slot], sem.at[0, slot]),
                pltpu.make_async_copy(pv_ref.at[layer, pid], vbuf.at[slot], sem.at[1, slot]))

    def start(b, j, slot):
        for cp in page_copies(b, j, slot):
            cp.start()

    def wait(b, j, slot):
        for cp in page_copies(b, j, slot):
            cp.wait()

    def live():
        return (jnp.max(c_scr[...]) > SB_STOP).astype(jnp.int32)

    start(0, 0, 0)

    def seq_body(b, carry):
        first = b % 2

        @pl.when(b + 1 < n_seq)
        def _():
            start(b + 1, 0, 1 - first)

        q_row = q_ref[b] * C_SCALE
        q_bd = jnp.where(own, jnp.broadcast_to(q_row, (C_HEADS, hd)), 0.0).astype(BF16)
        z_new = jnp.sum(jnp.where(own, q_row * kn_ref[b], 0.0), axis=1, keepdims=True)
        a_new = jnp.where(new_valid, jnp.exp(_log_sigmoid(z_new)), 0.0)
        c_scr[...] = jnp.where(new_valid, _log_sigmoid(-z_new), 0.0)
        acc_scr[...] = a_new * jnp.broadcast_to(vn_ref[b], acc_scr.shape)

        def process(slot):
            kt = kbuf[slot].astype(BF16)
            vt = vbuf[slot].astype(BF16)
            z = _dot(q_bd, kt)
            tail = _suffix_sums(_neg_softplus(z), ones_tri)
            a = jnp.exp(z + tail + c_scr[...])
            acc_scr[...] = acc_scr[...] + _dot_nt(a.astype(BF16), vt)
            c_scr[...] = c_scr[...] + tail[:, 0:1]

        wait(b, 0, first)
        if n_pages > 1:
            start(b, 1, 3)
        process(first)

        def cond(state):
            j, go = state
            return jnp.logical_and(j < n_pages, go > 0)

        def body(state):
            j, _ = state
            slot = 2 + j % 2
            wait(b, j, slot)

            @pl.when(j + 1 < n_pages)
            def _():
                start(b, j + 1, 2 + (j + 1) % 2)

            process(slot)
            return j + 1, live()

        j_end, _ = lax.while_loop(cond, body, (jnp.int32(1), live()))

        @pl.when(j_end < n_pages)
        def _():
            wait(b, j_end, 2 + j_end % 2)

        o_ref[b] = jnp.sum(jnp.where(own, acc_scr[...], 0.0), axis=0, keepdims=True)
        return carry

    lax.fori_loop(0, n_seq, seq_body, 0)


def sb_decode(q, k_new, v_new, pool_k, pool_v, layer, page_table):
    nb, n_pages = page_table.shape
    page = pool_k.shape[2]
    hd = C_HEADS * C_HEAD_DIM
    pk = jnp.transpose(pool_k, (0, 1, 3, 4, 2)).reshape(pool_k.shape[0], pool_k.shape[1], hd, page)
    pv = jnp.transpose(pool_v, (0, 1, 3, 4, 2)).reshape(pool_v.shape[0], pool_v.shape[1], hd, page)
    past_len = n_pages * page
    new_valid = past_len < past_len
    full = lambda i, pt: (0, 0, 0)
    grid_spec = pltpu.PrefetchScalarGridSpec(
        num_scalar_prefetch=1,
        grid=(1,),
        in_specs=[pl.BlockSpec((nb, 1, hd), full)] * 3 + [pl.BlockSpec(memory_space=pl.ANY)] * 2,
        out_specs=pl.BlockSpec((nb, 1, hd), full),
        scratch_shapes=[pltpu.VMEM((4, hd, page), F32), pltpu.VMEM((4, hd, page), F32),
                        pltpu.SemaphoreType.DMA((2, 4)),
                        pltpu.VMEM((C_HEADS, 1), F32), pltpu.VMEM((C_HEADS, hd), F32)])
    o = pl.pallas_call(
        functools.partial(_sb_decode_kernel, layer=layer, n_pages=n_pages, n_seq=nb, new_valid=new_valid),
        grid_spec=grid_spec,
        out_shape=jax.ShapeDtypeStruct((nb, 1, hd), F32),
        compiler_params=_params("arbitrary"),
        name="sb_decode",
    )(page_table.reshape(-1), q.reshape(nb, 1, hd), k_new.reshape(nb, 1, hd), v_new.reshape(nb, 1, hd), pk, pv)
    return o.reshape(nb, hd)


def _pad_heads(w, n_heads, width):
    k = w.shape[0]
    w = w.reshape(k, n_heads, width)
    return jnp.pad(w, ((0, 0), (0, 0), (0, HEAD_PAD - width))).reshape(k, n_heads * HEAD_PAD)


def _mla_weights(w_down, g_q, g_kv, w_uq, w_uk, w_uv, w_o):
    d_down = B_Q_LORA + B_KV_LORA + B_ROPE
    down_pad = -d_down % LANES
    place = jnp.zeros((LANES, B_HEADS, HEAD_PAD), F32)
    idx = jnp.arange(B_ROPE)
    place = place.at[idx, :, B_NOPE + idx].set(1.0).reshape(LANES, B_HEADS * HEAD_PAD)
    uk2 = w_uk.reshape(B_KV_LORA, B_HEADS * B_NOPE)
    uv2 = w_uv.reshape(B_KV_LORA, B_HEADS * B_VDIM)
    uv_pad = _pad_heads(uv2, B_HEADS, B_VDIM).astype(BF16)
    wo_pad = jnp.pad(w_o.reshape(B_HEADS, B_VDIM, D_MODEL), ((0, 0), (0, HEAD_PAD - B_VDIM), (0, 0)))
    return {
        "down": jnp.pad(w_down, ((0, 0), (0, down_pad))).astype(BF16),
        "g_q": g_q.reshape(1, B_Q_LORA), "g_kv": g_kv.reshape(1, B_KV_LORA),
        "uq": _pad_heads(w_uq, B_HEADS, B_NOPE + B_ROPE).astype(BF16),
        "uk": _pad_heads(uk2, B_HEADS, B_NOPE).astype(BF16),
        "rope_place": place.astype(BF16),
        "uv_t": uv_pad.T,
        "uk_t": jnp.pad(jnp.transpose(w_uk, (1, 2, 0)), ((0, 0), (0, HEAD_PAD - B_NOPE), (0, 0))).astype(BF16),
        "uv_h": jnp.transpose(uv_pad.reshape(B_KV_LORA, B_HEADS, HEAD_PAD), (1, 0, 2)),
        "o": wo_pad.reshape(B_HEADS * HEAD_PAD, D_MODEL).astype(BF16),
    }


def _mlstm_layer(xp, xs, g_pre, g_post, w_in, b_gate, g_head, w_out, state, batch, seq):
    hk, hv = A_HEADS * A_DK, A_HEADS * A_DV
    w_qkv = w_in[:, :2 * hk + hv].astype(BF16)
    w_o = w_in[:, 2 * hk + hv:2 * hk + 2 * hv].astype(BF16)
    w_g = jnp.pad(w_in[:, 2 * hk + 2 * hv:], ((0, 0), (0, LANES - 2 * A_HEADS))).astype(BF16)
    scale = jnp.concatenate([jnp.ones((hk,), F32), jnp.full((hk,), A_DK ** -0.5, F32), jnp.ones((hv,), F32)])
    w_out_b = w_out.astype(BF16)

    (qkv,), (og,), (gt,) = norm_linear(xp, g_pre, [(w_qkv, (BF16,), scale), (w_o, (F32,), None), (w_g, (F32,), None)])
    y, c_p, n_p, m_p = mlstm_prompt(qkv, og, gt, b_gate, g_head, batch, seq)
    xp = linear_norm_res(y, w_out_b, g_post, xp)

    (qkv_s,), (og_s,), (gt_s,) = norm_linear(
        xs, g_pre, [(w_qkv, (F32,), scale), (w_o, (F32,), None), (w_g, (F32,), None)])
    y_s, c_s, n_s, m_s = mlstm_decode(qkv_s[:, :hk], qkv_s[:, hk:2 * hk], qkv_s[:, 2 * hk:], og_s, gt_s,
                                      b_gate, g_head, *state)
    xs = linear_norm_res(y_s, w_out_b, g_post, xs)
    return xp, xs, (c_p, n_p, m_p), (c_s, n_s, m_s)


def _mla_layer(xp, xs, g_pre, g_post, w, pool_lat, pool_kr, layer, page_table, batch, seq):
    nb, n_pages = page_table.shape
    past_len = n_pages * pool_lat.shape[2]
    t = _tile(seq, ATT_TILE)
    tabs_p = _rope_tables(jnp.arange(seq, dtype=jnp.int32))
    ckv_p, kr_p, q, kf, vt = mla_proj(xp, g_pre, w, tabs_p, seq // t, True, BF16, t)
    o = mla_attn(q, kf, vt, batch, seq, t)
    xp = linear_norm_res(o, w["o"], g_post, xp)

    tabs_s = _rope_tables(jnp.full((nb,), past_len, jnp.int32))
    ckv_s, kr_s, q_s = mla_proj(xs, g_pre, w, tabs_s, 1, False, BF16, nb)
    q_lat = head_matmul(q_s, w["uk_t"], BF16).reshape(nb, B_HEADS, B_KV_LORA)
    q_rope = q_s.reshape(nb, B_HEADS, HEAD_PAD)[:, :, B_NOPE:B_NOPE + B_ROPE]
    o_lat = mla_decode(q_lat, q_rope, ckv_s, kr_s, pool_lat, pool_kr, layer, page_table)
    o_s = head_matmul(o_lat.reshape(nb, B_HEADS * B_KV_LORA), w["uv_h"], BF16)
    xs = linear_norm_res(o_s, w["o"], g_post, xs)
    return xp, xs, (ckv_p, kr_p), (ckv_s, kr_s)


def _sb_layer(xp, xs, g_pre, g_post, w_qkv, w_o, pool_k, pool_v, layer, page_table, batch, seq):
    hd = C_HEADS * C_HEAD_DIM
    w_q = w_qkv[:, :hd].astype(BF16)
    w_k = w_qkv[:, hd:2 * hd].astype(BF16)
    w_v = w_qkv[:, 2 * hd:].astype(BF16)
    w_o_b = w_o.astype(BF16)
    (q,), (k32_t, k16), (v32_t, v16) = norm_linear(
        xp, g_pre, [(w_q, (BF16,), None), (w_k, (("T", F32), BF16), None), (w_v, (("T", F32), BF16), None)], seq=seq)
    k32 = jnp.transpose(k32_t.reshape(batch, C_HEADS, C_HEAD_DIM, seq), (0, 3, 1, 2))
    v32 = jnp.transpose(v32_t.reshape(batch, C_HEADS, C_HEAD_DIM, seq), (0, 3, 1, 2))
    o = sb_attn(q, k16, v16, batch, seq)
    xp = linear_norm_res(o, w_o_b, g_post, xp)

    (q_s,), (k_s,), (v_s,) = norm_linear(xs, g_pre, [(w_q, (F32,), None), (w_k, (F32,), None), (w_v, (F32,), None)])
    o_s = sb_decode(q_s, k_s, v_s, pool_k, pool_v, layer, page_table)
    xs = linear_norm_res(o_s, w_o_b, g_post, xs)
    return xp, xs, (k32, v32), (k_s, v_s)


def kernel(x_prompt, x_sample, state_mlstm_C, state_mlstm_n, state_mlstm_m, cache_mla_latent, cache_mla_krope, cache_sb_k, cache_sb_v, page_table, g_mix_pre, g_mix_post, g_ffn_pre, g_ffn_post, w_ffn_in, w_ffn_out, w_a_in, b_a_gate, g_a_head, w_a_out, w_b_down, g_b_q, g_b_kv, w_b_uq, w_b_uk, w_b_uv, w_b_o, w_c_qkv, w_c_o):
    batch, seq, d = x_prompt.shape
    nb = x_sample.shape[0]
    depth = g_mix_pre.shape[0]
    xp = x_prompt.reshape(batch * seq, d)
    xs = x_sample.reshape(nb, d)
    a_p, a_s, b_p, b_s, c_p, c_s = [], [], [], [], [], []
    for i in range(depth):
        kind, j = i % N_MIXERS, i // N_MIXERS
        if kind == 0:
            state = (state_mlstm_C, state_mlstm_n, state_mlstm_m[j], j)
            xp, xs, st_p, st_s = _mlstm_layer(xp, xs, g_mix_pre[i], g_mix_post[i], w_a_in[j], b_a_gate[j],
                                              g_a_head[j], w_a_out[j], state, batch, seq)
            a_p.append(st_p)
            a_s.append(st_s)
        elif kind == 1:
            w = _mla_weights(w_b_down[j], g_b_q[j], g_b_kv[j], w_b_uq[j], w_b_uk[j], w_b_uv[j], w_b_o[j])
            xp, xs, st_p, st_s = _mla_layer(xp, xs, g_mix_pre[i], g_mix_post[i], w, cache_mla_latent,
                                            cache_mla_krope, j, page_table, batch, seq)
            b_p.append(st_p)
            b_s.append(st_s)
        else:
            xp, xs, st_p, st_s = _sb_layer(xp, xs, g_mix_pre[i], g_mix_post[i], w_c_qkv[j], w_c_o[j],
                                           cache_sb_k, cache_sb_v, j, page_table, batch, seq)
            c_p.append(st_p)
            c_s.append(st_s)
        w_gate = w_ffn_in[i][:, :D_FF].astype(BF16)
        w_up = w_ffn_in[i][:, D_FF:].astype(BF16)
        w_out = w_ffn_out[i].astype(BF16)
        xp = ffn(xp, g_ffn_pre[i], w_gate, w_up, w_out, g_ffn_post[i])
        xs = ffn(xs, g_ffn_pre[i], w_gate, w_up, w_out, g_ffn_post[i])

    def stack(states, idx, shape):
        return jnp.stack([s[idx].reshape(shape) for s in states])

    return (xp.reshape(batch, seq, d), xs.reshape(nb, 1, d),
            stack(a_p, 0, (batch, A_HEADS, A_DK, A_DV)), stack(a_p, 1, (batch, A_HEADS, A_DK)),
            stack(a_p, 2, (batch, A_HEADS)),
            stack(a_s, 0, (nb, A_HEADS, A_DK, A_DV)), stack(a_s, 1, (nb, A_HEADS, A_DK)),
            stack(a_s, 2, (nb, A_HEADS)),
            stack(b_p, 0, (batch, seq, B_KV_LORA)), stack(b_p, 1, (batch, seq, B_ROPE)),
            stack(b_s, 0, (nb, 1, B_KV_LORA)), stack(b_s, 1, (nb, 1, B_ROPE)),
            stack(c_p, 0, (batch, seq, C_HEADS, C_HEAD_DIM)), stack(c_p, 1, (batch, seq, C_HEADS, C_HEAD_DIM)),
            stack(c_s, 0, (nb, 1, C_HEADS, C_HEAD_DIM)), stack(c_s, 1, (nb, 1, C_HEADS, C_HEAD_DIM)))
```
